```python
import math
import jax, jax.numpy as jnp
from jax import lax
import numpy as np

D_MODEL = 1024
BATCH = 2
SEQ = 16384
DEPTH = 4
DEC_BATCH = 16
DEC_SEQ = 2048
PAST_LEN = 128

N_MIXERS = 3
N_CONV_LAYERS = (DEPTH + 2) // N_MIXERS
N_DELTA_LAYERS = (DEPTH + 1) // N_MIXERS
N_ATTN_LAYERS = DEPTH // N_MIXERS

RMS_EPS = 1e-6
LN_EPS = 1e-5
NEG_INF = -1e30

CONV_WIDTH = 31

DN_HEAD_DIM = 128
DN_HEADS = D_MODEL // DN_HEAD_DIM
DN_WIDTH = DN_HEADS * DN_HEAD_DIM
SHORT_CONV = 5
CHUNK = 64

ATTN_GROUPS = ((128, 1), (512, 4), (2048, 16))
ATTN_HEAD_DIM = 128
ATTN_HEADS = D_MODEL // ATTN_HEAD_DIM
ATTN_WIDTH = ATTN_HEADS * ATTN_HEAD_DIM

N_EXPERTS = 16
EXPERT_FF = 2 * D_MODEL
CAPACITY_FACTOR = 2

kernel_name = "hybrid_bidir_conv_deltanet_dilated_ecmoe"


def rmsnorm(x, g):
    xf = x.astype(jnp.float32)
    y = xf * lax.rsqrt(jnp.mean(xf * xf, axis=-1, keepdims=True) + RMS_EPS)
    return (y * g.astype(jnp.float32)).astype(x.dtype)


def l2norm(x):
    return x * lax.rsqrt(jnp.sum(x * x, axis=-1, keepdims=True) + RMS_EPS)


def depthwise_conv(x, w):
    width, ch = w.shape
    left = (width - 1) // 2
    return lax.conv_general_dilated(
        x, w[:, None, :].astype(x.dtype), window_strides=(1,),
        padding=[(left, width - 1 - left)],
        dimension_numbers=('NWC', 'WIO', 'NWC'), feature_group_count=ch)


def conformer_conv(h, w_glu, b_glu, w_dw, b_dw, ln_g, ln_b, w_out, b_out):
    u = h @ w_glu + b_glu
    a, gate = jnp.split(u, 2, axis=-1)
    u = a * jax.nn.sigmoid(gate)
    u = depthwise_conv(u, w_dw) + b_dw
    uf = u.astype(jnp.float32)
    mu = jnp.mean(uf, axis=-1, keepdims=True)
    var = jnp.mean(jnp.square(uf - mu), axis=-1, keepdims=True)
    u = ((uf - mu) * lax.rsqrt(var + LN_EPS) * ln_g + ln_b).astype(h.dtype)
    return jax.nn.silu(u) @ w_out + b_out


def chunked_gated_delta(q, k, v, g, beta):
    B, S, H, dk = q.shape
    dv = v.shape[-1]
    nc = S // CHUNK

    def blocks(t):
        t = t.reshape((B, nc, CHUNK, H) + t.shape[3:])
        return jnp.moveaxis(t, 3, 1)

    q, k, v, g, beta = blocks(q), blocks(k), blocks(v), blocks(g), blocks(beta)
    g = jnp.cumsum(g, axis=-1)
    pos = jnp.arange(CHUNK)
    incl = pos[:, None] >= pos[None, :]
    strict = pos[:, None] > pos[None, :]
    decay = jnp.exp(jnp.where(incl, g[..., :, None] - g[..., None, :], NEG_INF))
    kb = k * beta[..., None]
    lower = jnp.where(strict, jnp.einsum('bhncd,bhnsd->bhncs', kb, k) * decay, 0.0)
    rhs = jnp.concatenate([v * beta[..., None], kb * jnp.exp(g)[..., None]], axis=-1)
    sol = lax.linalg.triangular_solve(lower, rhs, left_side=True, lower=True, unit_diagonal=True)
    u, w = sol[..., :dv], sol[..., dv:]
    qk = jnp.where(incl, jnp.einsum('bhncd,bhnsd->bhncs', q, k) * decay, 0.0)
    q_dec = q * jnp.exp(g)[..., None]
    g_last = g[..., -1]
    k_dec = k * jnp.exp(g_last[..., None] - g)[..., None]

    def step(state, inp):
        q_c, k_c, u_c, w_c, qk_c, gl_c = inp
        v_new = u_c - jnp.einsum('bhcd,bhde->bhce', w_c, state)
        o = jnp.einsum('bhcd,bhde->bhce', q_c, state) + jnp.einsum('bhcs,bhse->bhce', qk_c, v_new)
        state = state * jnp.exp(gl_c)[..., None, None] + jnp.einsum('bhcd,bhce->bhde', k_c, v_new)
        return state, o

    xs = tuple(jnp.moveaxis(t, 2, 0) for t in (q_dec, k_dec, u, w, qk, g_last))
    state0 = jnp.zeros((B, H, dk, dv), jnp.float32)
    _, o = lax.scan(step, state0, xs)
    o = jnp.moveaxis(o, 0, 2)
    return jnp.moveaxis(o, 1, 3).reshape(B, S, H, dv)


def gated_deltanet(h, w_in, w_conv, a_log, dt_bias, norm_g, w_out):
    B, S, _ = h.shape
    H, dh = DN_HEADS, DN_HEAD_DIM
    proj = h @ w_in
    qkv = jax.nn.silu(depthwise_conv(proj[..., :3 * DN_WIDTH], w_conv))
    qkv = qkv.astype(jnp.float32).reshape(B, S, 3, H, dh)
    z = proj[..., 3 * DN_WIDTH:4 * DN_WIDTH].astype(jnp.float32).reshape(B, S, H, dh)
    ba = proj[..., 4 * DN_WIDTH:].astype(jnp.float32).reshape(B, S, 2, 2, H)
    beta = jax.nn.sigmoid(ba[:, :, 0])
    g = -jnp.exp(a_log.astype(jnp.float32)) * jax.nn.softplus(ba[:, :, 1] + dt_bias)
    q = l2norm(qkv[:, :, 0]) * (dh ** -0.5)
    k = l2norm(qkv[:, :, 1])
    v = qkv[:, :, 2]
    o_fwd = chunked_gated_delta(q, k, v, g[:, :, 0], beta[:, :, 0])
    rev = lambda t: jnp.flip(t, axis=1)
    o_bwd = rev(chunked_gated_delta(rev(q), rev(k), rev(v), rev(g[:, :, 1]), rev(beta[:, :, 1])))
    o = o_fwd + o_bwd
    o = o * lax.rsqrt(jnp.mean(o * o, axis=-1, keepdims=True) + RMS_EPS) * norm_g * jax.nn.silu(z)
    return o.reshape(B, S, DN_WIDTH).astype(h.dtype) @ w_out


def alibi_slopes(n_heads):
    return jnp.exp2(-8.0 * jnp.arange(1, n_heads + 1, dtype=jnp.float32) / n_heads)


def dilated_window_attention(q, k, v, dilation, half, slopes):
    B, S, H, dh = q.shape
    span = dilation * half
    s_pad = -(-S // span) * span
    L = s_pad // dilation
    nb = L // half

    def to_blocks(t):
        t = jnp.pad(t.astype(jnp.float32), ((0, 0), (0, s_pad - S), (0, 0), (0, 0)))
        t = jnp.swapaxes(t.reshape(B, L, dilation, H, dh), 1, 2)
        return t.reshape(B, dilation, nb, half, H, dh)

    def with_neighbours(t):
        tp = jnp.pad(t, ((0, 0), (0, 0), (1, 1), (0, 0), (0, 0), (0, 0)))
        return jnp.concatenate([tp[:, :, :-2], tp[:, :, 1:-1], tp[:, :, 2:]], axis=3)

    qb = to_blocks(q)
    kn = with_neighbours(to_blocks(k))
    vn = with_neighbours(to_blocks(v))
    rel = jnp.arange(3 * half)[None, :] - half - jnp.arange(half)[:, None]
    sub = jnp.arange(nb)[:, None] * half + jnp.arange(3 * half)[None, :] - half
    key_pos = jnp.arange(dilation)[:, None, None] + dilation * sub[None]
    key_ok = (sub[None] >= 0) & (key_pos < S)
    mask = (jnp.abs(rel) <= half)[None, None] & key_ok[:, :, None, :]
    scores = jnp.einsum('brnqhd,brnkhd->brnhqk', qb, kn) * (dh ** -0.5)
    penalty = slopes[:, None, None] * (dilation * jnp.abs(rel)).astype(jnp.float32)
    scores = jnp.where(mask[None, :, :, None], scores - penalty, NEG_INF)
    m = jnp.max(scores, axis=-1, keepdims=True)
    p = jnp.exp(scores - m)
    den = jnp.sum(p, axis=-1, keepdims=True)
    o = jnp.einsum('brnhqk,brnkhd->brnqhd', p / den, vn)
    lse = jnp.moveaxis((m + jnp.log(den))[..., 0], 3, 4)

    def from_blocks(t):
        t = t.reshape((B, dilation, L) + t.shape[4:])
        t = jnp.swapaxes(t, 1, 2)
        return t.reshape((B, s_pad) + t.shape[3:])[:, :S]

    return from_blocks(o), from_blocks(lse)


def dilated_attention_mixer(h, w_in, w_out):
    B, S, _ = h.shape
    proj = (h @ w_in).reshape(B, S, len(ATTN_GROUPS), 3, ATTN_HEADS, ATTN_HEAD_DIM)
    slopes = alibi_slopes(ATTN_HEADS)
    outs, lses = [], []
    for gi, (window, dilation) in enumerate(ATTN_GROUPS):
        o, lse = dilated_window_attention(proj[:, :, gi, 0], proj[:, :, gi, 1], proj[:, :, gi, 2],
                                          dilation, window // (2 * dilation), slopes)
        outs.append(o)
        lses.append(lse)
    wts = jax.nn.softmax(jnp.stack(lses), axis=0)
    o = jnp.sum(wts[..., None] * jnp.stack(outs), axis=0)
    return o.reshape(B, S, ATTN_WIDTH).astype(h.dtype) @ w_out


def expert_choice_moe(x, router, w_gate, w_up, w_down):
    n, d = x.shape
    cap = max(1, CAPACITY_FACTOR * n // N_EXPERTS)
    affinity = jax.nn.softmax((x @ router).astype(jnp.float32), axis=-1)
    gate, idx = lax.top_k(affinity.T, cap)
    xe = x[idx]
    hid = jax.nn.silu(jnp.einsum('ecd,edf->ecf', xe, w_gate)) * jnp.einsum('ecd,edf->ecf', xe, w_up)
    ye = jnp.einsum('ecf,efd->ecd', hid, w_down) * gate[..., None].astype(x.dtype)
    return jnp.zeros_like(x).at[idx.reshape(-1)].add(ye.reshape(-1, d))


def setup_inputs(seed: int = 0) -> dict:
    key = jax.random.key(seed)
    keys = iter(jax.random.split(key, 40))
    f32 = jnp.float32
    D, E, F = D_MODEL, N_EXPERTS, EXPERT_FF
    nA, nB, nC = N_CONV_LAYERS, N_DELTA_LAYERS, N_ATTN_LAYERS

    def normal(shape, scale):
        return jax.random.normal(next(keys), shape, f32) * scale

    def gain(shape):
        return 1.0 + normal(shape, 0.02)

    x_prompt = normal((BATCH, SEQ, D), 1.0)
    x_sample = normal((DEC_BATCH, DEC_SEQ, D), 1.0)
    norm_mix = gain((DEPTH, D))
    norm_ffn = gain((DEPTH, D))
    norm_final = gain((D,))
    conv_w_glu = normal((nA, D, 2 * D), D ** -0.5)
    conv_b_glu = normal((nA, 2 * D), 0.02)
    conv_w_dw = normal((nA, CONV_WIDTH, D), CONV_WIDTH ** -0.5)
    conv_b_dw = normal((nA, D), 0.02)
    conv_ln_g = gain((nA, D))
    conv_ln_b = normal((nA, D), 0.02)
    conv_w_out = normal((nA, D, D), D ** -0.5)
    conv_b_out = normal((nA, D), 0.02)
    dn_w_in = normal((nB, D, 4 * DN_WIDTH + 4 * DN_HEADS), D ** -0.5)
    dn_w_conv = normal((nB, SHORT_CONV, 3 * DN_WIDTH), SHORT_CONV ** -0.5)
    dn_a_log = jnp.log(jax.random.uniform(next(keys), (nB, 2, DN_HEADS), f32, 1.0, 16.0))
    dt = jnp.exp(jax.random.uniform(next(keys), (nB, 2, DN_HEADS), f32, math.log(1e-3), math.log(1e-1)))
    dn_dt_bias = dt + jnp.log(-jnp.expm1(-dt))
    dn_norm_g = gain((nB, DN_HEAD_DIM))
    dn_w_out = normal((nB, DN_WIDTH, D), DN_WIDTH ** -0.5)
    attn_w_in = normal((nC, D, len(ATTN_GROUPS) * 3 * ATTN_WIDTH), D ** -0.5)
    attn_w_out = normal((nC, ATTN_WIDTH, D), ATTN_WIDTH ** -0.5)
    moe_router = normal((DEPTH, D, E), D ** -0.5)
    moe_w_gate = normal((DEPTH, E, D, F), D ** -0.5)
    moe_w_up = normal((DEPTH, E, D, F), D ** -0.5)
    moe_w_down = normal((DEPTH, E, F, D), F ** -0.5)
    return {"x_prompt": x_prompt, "x_sample": x_sample, "norm_mix": norm_mix, "norm_ffn": norm_ffn,
            "norm_final": norm_final, "conv_w_glu": conv_w_glu, "conv_b_glu": conv_b_glu,
            "conv_w_dw": conv_w_dw, "conv_b_dw": conv_b_dw, "conv_ln_g": conv_ln_g, "conv_ln_b": conv_ln_b,
            "conv_w_out": conv_w_out, "conv_b_out": conv_b_out, "dn_w_in": dn_w_in, "dn_w_conv": dn_w_conv,
            "dn_a_log": dn_a_log, "dn_dt_bias": dn_dt_bias, "dn_norm_g": dn_norm_g, "dn_w_out": dn_w_out,
            "attn_w_in": attn_w_in, "attn_w_out": attn_w_out, "moe_router": moe_router,
            "moe_w_gate": moe_w_gate, "moe_w_up": moe_w_up, "moe_w_down": moe_w_down}


def reference(x_prompt, x_sample, norm_mix, norm_ffn, norm_final, conv_w_glu, conv_b_glu, conv_w_dw,
              conv_b_dw, conv_ln_g, conv_ln_b, conv_w_out, conv_b_out, dn_w_in, dn_w_conv, dn_a_log,
              dn_dt_bias, dn_norm_g, dn_w_out, attn_w_in, attn_w_out, moe_router, moe_w_gate, moe_w_up,
              moe_w_down):
    def trunk(x):
        b, s, d = x.shape
        for i in range(DEPTH):
            kind, j = i % N_MIXERS, i // N_MIXERS
            h = rmsnorm(x, norm_mix[i])
            if kind == 0:
                mix = conformer_conv(h, conv_w_glu[j], conv_b_glu[j], conv_w_dw[j], conv_b_dw[j],
                                     conv_ln_g[j], conv_ln_b[j], conv_w_out[j], conv_b_out[j])
            elif kind == 1:
                mix = gated_deltanet(h, dn_w_in[j], dn_w_conv[j], dn_a_log[j], dn_dt_bias[j],
                                     dn_norm_g[j], dn_w_out[j])
            else:
                mix = dilated_attention_mixer(h, attn_w_in[j], attn_w_out[j])
            x = x + mix
            h = rmsnorm(x, norm_ffn[i]).reshape(b * s, d)
            x = x + expert_choice_moe(h, moe_router[i], moe_w_gate[i], moe_w_up[i], moe_w_down[i]).reshape(b, s, d)
        return rmsnorm(x, norm_final)

    y_prompt = trunk(x_prompt)
    y_sample = trunk(x_sample)
    return (y_prompt, y_sample)
```

```python
import functools
import math

import jax
import jax.numpy as jnp
from jax import lax
from jax.experimental import pallas as pl
from jax.experimental.pallas import tpu as pltpu

F32 = jnp.float32
BF16 = jnp.bfloat16
I32 = jnp.int32

D_MODEL = 1024
DEPTH = 4
N_MIXERS = 3
RMS_EPS = 1e-6
LN_EPS = 1e-5
NEG_INF = -1e30
CONV_WIDTH = 31
HEAD_DIM = 128
N_HEADS = D_MODEL // HEAD_DIM
SHORT_CONV = 5
ATTN_GROUPS = ((128, 1), (512, 4), (2048, 16))
N_EXPERTS = 16
EXPERT_FF = 2 * D_MODEL
CAPACITY_FACTOR = 2

VMEM_LIMIT_BYTES = 56 * 1024 * 1024
LANES = 128
BF16_SUBLANES = 16

DN_CHUNK = 128
ATTN_HALF = 64
ATTN_TQ = 128
FF_CHUNK = 512
COMBINE_TILE = 256
COMBINE_KCHUNK = 256


def _cparams(*sem):
    return pltpu.CompilerParams(dimension_semantics=sem, vmem_limit_bytes=VMEM_LIMIT_BYTES)


def _rms(x, g):
    return x * lax.rsqrt(jnp.mean(x * x, axis=-1, keepdims=True) + RMS_EPS) * g


def _silu(x):
    return x * jax.nn.sigmoid(x)


def _dot(a, b):
    return jnp.dot(a, b, preferred_element_type=F32)


def _dot_nt(a, b):
    return lax.dot_general(a, b, (((1,), (1,)), ((), ())), preferred_element_type=F32)


def _dot_tn(a, b):
    return lax.dot_general(a, b, (((0,), (0,)), ((), ())), preferred_element_type=F32)


def _dot3(a, b, dot=_dot):
    ah = a.astype(BF16)
    al = (a - ah.astype(F32)).astype(BF16)
    bh = b.astype(BF16)
    bl = (b - bh.astype(F32)).astype(BF16)
    return dot(ah, bh) + (dot(ah, bl) + dot(al, bh))


def _norm_linear_kernel(x_ref, g_ref, w_ref, o_ref, h_ref):
    @pl.when(pl.program_id(1) == 0)
    def _():
        h_ref[...] = _rms(x_ref[...], g_ref[...]).astype(BF16)

    o_ref[...] = _dot(h_ref[...], w_ref[...]).astype(o_ref.dtype)


def norm_linear(x, g, w, out_dtype, tm=1024, tn=512):
    n, d = x.shape
    nout = w.shape[1]
    return pl.pallas_call(
        _norm_linear_kernel,
        grid=(n // tm, nout // tn),
        in_specs=[pl.BlockSpec((tm, d), lambda i, j: (i, 0)),
                  pl.BlockSpec((1, d), lambda i, j: (0, 0)),
                  pl.BlockSpec((d, tn), lambda i, j: (0, j))],
        out_specs=pl.BlockSpec((tm, tn), lambda i, j: (i, j)),
        out_shape=jax.ShapeDtypeStruct((n, nout), out_dtype),
        scratch_shapes=[pltpu.VMEM((tm, d), BF16)],
        compiler_params=_cparams("parallel", "arbitrary"),
        name="norm_linear",
    )(x, g.reshape(1, d), w)


def _norm_glu_kernel(x_ref, g_ref, wa_ref, wg_ref, ba_ref, bg_ref, o_ref, h_ref):
    @pl.when(pl.program_id(1) == 0)
    def _():
        h_ref[...] = _rms(x_ref[...], g_ref[...]).astype(BF16)

    h = h_ref[...]
    a = _dot(h, wa_ref[...]) + ba_ref[...]
    gate = _dot(h, wg_ref[...]) + bg_ref[...]
    o_ref[...] = a * jax.nn.sigmoid(gate)


def norm_glu(x, g, w, b, tm=1024, tn=512):
    n, d = x.shape
    nb = d // tn
    b2 = b.reshape(1, 2 * d)
    return pl.pallas_call(
        _norm_glu_kernel,
        grid=(n // tm, nb),
        in_specs=[pl.BlockSpec((tm, d), lambda i, j: (i, 0)),
                  pl.BlockSpec((1, d), lambda i, j: (0, 0)),
                  pl.BlockSpec((d, tn), lambda i, j: (0, j)),
                  pl.BlockSpec((d, tn), lambda i, j: (0, j + nb)),
                  pl.BlockSpec((1, tn), lambda i, j: (0, j)),
                  pl.BlockSpec((1, tn), lambda i, j: (0, j + nb))],
        out_specs=pl.BlockSpec((tm, tn), lambda i, j: (i, j)),
        out_shape=jax.ShapeDtypeStruct((n, d), F32),
        scratch_shapes=[pltpu.VMEM((tm, d), BF16)],
        compiler_params=_cparams("parallel", "arbitrary"),
        name="norm_glu",
    )(x, g.reshape(1, d), w, w, b2, b2)


CONV_HALO = 16


def _conv_out_kernel(up_ref, uc_ref, un_ref, wdw_ref, bdw_ref, lng_ref, lnb_ref, wo_ref, bo_ref, x_ref,
                     o_ref, win_ref, *, ts):
    i = pl.program_id(1)
    last = pl.num_programs(1) - 1
    win_ref[0:CONV_HALO, :] = jnp.where(i > 0, up_ref[...], 0.0)
    win_ref[CONV_HALO:CONV_HALO + ts, :] = uc_ref[...]
    win_ref[CONV_HALO + ts:, :] = jnp.where(i < last, un_ref[...], 0.0)
    acc = jnp.zeros((ts, D_MODEL), F32) + bdw_ref[...]
    for s in range(8):
        ws = win_ref[pl.ds(s, ts + 24), :]
        for a in range(4):
            j = 8 * a + s - 1
            if 0 <= j < CONV_WIDTH:
                acc = acc + wdw_ref[j:j + 1, :] * ws[8 * a:8 * a + ts, :]
    mu = jnp.mean(acc, axis=-1, keepdims=True)
    cen = acc - mu
    var = jnp.mean(cen * cen, axis=-1, keepdims=True)
    u = cen * lax.rsqrt(var + LN_EPS) * lng_ref[...] + lnb_ref[...]
    y = _silu(u).astype(BF16)
    o_ref[...] = x_ref[...] + _dot(y, wo_ref[...]) + bo_ref[...]


def conv_out(u, x, w_dw, b_dw, ln_g, ln_b, w_out, b_out, ts=256):
    b, s, d = u.shape
    hb = ts // CONV_HALO
    nblk = s // CONV_HALO
    wdw = jnp.concatenate([w_dw, jnp.zeros((1, d), F32)], axis=0)
    row = lambda v: v.reshape(1, d)
    return pl.pallas_call(
        functools.partial(_conv_out_kernel, ts=ts),
        grid=(b, s // ts),
        in_specs=[pl.BlockSpec((None, CONV_HALO, d), lambda bi, i: (bi, jnp.maximum(i * hb - 1, 0), 0)),
                  pl.BlockSpec((None, ts, d), lambda bi, i: (bi, i, 0)),
                  pl.BlockSpec((None, CONV_HALO, d), lambda bi, i: (bi, jnp.minimum((i + 1) * hb, nblk - 1), 0)),
                  pl.BlockSpec((CONV_WIDTH + 1, d), lambda bi, i: (0, 0)),
                  pl.BlockSpec((1, d), lambda bi, i: (0, 0)),
                  pl.BlockSpec((1, d), lambda bi, i: (0, 0)),
                  pl.BlockSpec((1, d), lambda bi, i: (0, 0)),
                  pl.BlockSpec((d, d), lambda bi, i: (0, 0)),
                  pl.BlockSpec((1, d), lambda bi, i: (0, 0)),
                  pl.BlockSpec((None, ts, d), lambda bi, i: (bi, i, 0))],
        out_specs=pl.BlockSpec((None, ts, d), lambda bi, i: (bi, i, 0)),
        out_shape=jax.ShapeDtypeStruct((b, s, d), F32),
        scratch_shapes=[pltpu.VMEM((ts + 2 * CONV_HALO, d), F32)],
        compiler_params=_cparams("parallel", "parallel"),
        name="conv_out",
    )(u, u, u, wdw, row(b_dw), row(ln_g), row(ln_b), w_out, row(b_out), x)


DN_HALO = 8
DN_BG = 32


def _dn_prep_kernel(pp_ref, pc_ref, pn_ref, ba_ref, wc_ref, alog_ref, dtb_ref, eye_ref,
                    q_ref, k_ref, v_ref, bg_ref, bgt_ref, win_ref, *, ts):
    i = pl.program_id(1)
    last = pl.num_programs(1) - 1
    w3 = 3 * D_MODEL
    win_ref[0:DN_HALO, :] = jnp.where(i > 0, pp_ref[...], 0.0)
    win_ref[DN_HALO:DN_HALO + ts, :] = pc_ref[...]
    win_ref[DN_HALO + ts:, :] = jnp.where(i < last, pn_ref[...], 0.0)
    left = (SHORT_CONV - 1) // 2
    acc = jnp.zeros((ts, w3), F32)
    for j in range(SHORT_CONV):
        acc = acc + wc_ref[j:j + 1, :] * win_ref[pl.ds(DN_HALO - left + j, ts), :]
    qkv = _silu(acc)

    def l2n(t):
        return t * lax.rsqrt(jnp.sum(t * t, axis=-1, keepdims=True) + RMS_EPS)

    for h in range(N_HEADS):
        sl = slice(h * HEAD_DIM, (h + 1) * HEAD_DIM)
        q_ref[:, sl] = l2n(qkv[:, sl]) * (HEAD_DIM ** -0.5)
        k_ref[:, sl] = l2n(qkv[:, D_MODEL + h * HEAD_DIM:D_MODEL + (h + 1) * HEAD_DIM])
    v_ref[...] = qkv[:, 2 * D_MODEL:]
    ba = ba_ref[:, :DN_BG]
    lane = lax.broadcasted_iota(I32, (ts, DN_BG), 1)
    beta = jax.nn.sigmoid(ba)
    g = -jnp.exp(alog_ref[...]) * jax.nn.softplus(ba + dtb_ref[...])
    bg = jnp.where(lane < DN_BG // 2, beta, g)
    bg_ref[...] = bg
    bgt_ref[...] = lax.dot_general(eye_ref[...], bg, (((1,), (1,)), ((), ())),
                                   precision=lax.Precision.HIGHEST, preferred_element_type=F32)


def dn_prep(proj, w_conv, a_log, dt_bias, ts=256):
    b, s, _ = proj.shape
    d = D_MODEL
    hb = ts // DN_HALO
    nblk = s // DN_HALO
    zeros16 = jnp.zeros((16,), F32)
    alog = jnp.concatenate([zeros16, a_log.reshape(-1)]).reshape(1, DN_BG)
    dtb = jnp.concatenate([zeros16, dt_bias.reshape(-1)]).reshape(1, DN_BG)
    eye = jnp.eye(DN_BG, dtype=F32)
    bacol = 4 * d // LANES
    tile = lambda bi, i: (bi, i, 0)
    return pl.pallas_call(
        functools.partial(_dn_prep_kernel, ts=ts),
        grid=(b, s // ts),
        in_specs=[pl.BlockSpec((None, DN_HALO, 3 * d), lambda bi, i: (bi, jnp.maximum(i * hb - 1, 0), 0)),
                  pl.BlockSpec((None, ts, 3 * d), tile),
                  pl.BlockSpec((None, DN_HALO, 3 * d), lambda bi, i: (bi, jnp.minimum((i + 1) * hb, nblk - 1), 0)),
                  pl.BlockSpec((None, ts, LANES), lambda bi, i: (bi, i, bacol)),
                  pl.BlockSpec((SHORT_CONV, 3 * d), lambda bi, i: (0, 0)),
                  pl.BlockSpec((1, DN_BG), lambda bi, i: (0, 0)),
                  pl.BlockSpec((1, DN_BG), lambda bi, i: (0, 0)),
                  pl.BlockSpec((DN_BG, DN_BG), lambda bi, i: (0, 0))],
        out_specs=[pl.BlockSpec((None, ts, d), tile),
                   pl.BlockSpec((None, ts, d), tile),
                   pl.BlockSpec((None, ts, d), tile),
                   pl.BlockSpec((None, ts, DN_BG), tile),
                   pl.BlockSpec((None, DN_BG, ts), lambda bi, i: (bi, 0, i))],
        out_shape=[jax.ShapeDtypeStruct((b, s, d), F32)] * 3
                  + [jax.ShapeDtypeStruct((b, s, DN_BG), F32), jax.ShapeDtypeStruct((b, DN_BG, s), F32)],
        scratch_shapes=[pltpu.VMEM((ts + 2 * DN_HALO, 3 * d), F32)],
        compiler_params=_cparams("parallel", "parallel"),
        name="dn_prep",
    )(proj, proj, proj, proj, w_conv, alog, dtb, eye)


def _delta_chunk(q, k, v, beta, gc_col, gc_row, g_tot, state, reverse):
    c = q.shape[0]
    r = lax.broadcasted_iota(I32, (c, c), 0)
    s = lax.broadcasted_iota(I32, (c, c), 1)
    incl = (r <= s) if reverse else (r >= s)
    strict = (r < s) if reverse else (r > s)
    decay = jnp.exp(jnp.where(incl, gc_col - gc_row, NEG_INF))
    kb = k * beta
    m = jnp.where(strict, -_dot3(kb, k, _dot_nt) * decay, 0.0)
    sol = jnp.concatenate([v * beta, kb * jnp.exp(gc_col)], axis=1)
    n_levels = max(1, (c - 1).bit_length())
    for lvl in range(n_levels):
        sol = sol + _dot3(m, sol)
        if lvl + 1 < n_levels:
            m = _dot3(m, m)
    dv = v.shape[1]
    u, w = sol[:, :dv], sol[:, dv:]
    a = jnp.where(incl, _dot3(q, k, _dot_nt) * decay, 0.0)
    v_new = u - _dot3(w, state)
    o = _dot3(q * jnp.exp(gc_col), state) + _dot3(a, v_new)
    k_dec = k * jnp.exp(g_tot - gc_col)
    state = state * jnp.exp(g_tot) + _dot3(k_dec, v_new, _dot_tn)
    return o, state


def _delta_kernel(qf_ref, kf_ref, vf_ref, bgf_ref, bgtf_ref, qb_ref, kb_ref, vb_ref, bgb_ref, bgtb_ref,
                  of_ref, ob_ref, state_ref, *, chunk):
    @pl.when(pl.program_id(1) == 0)
    def _():
        state_ref[...] = jnp.zeros_like(state_ref)

    c = chunk
    r = lax.broadcasted_iota(I32, (c, c), 0)
    s = lax.broadcasted_iota(I32, (c, c), 1)
    hi = lax.Precision.HIGHEST
    for di, (q_ref, k_ref, v_ref, bg_ref, bgt_ref, o_ref) in enumerate(
            ((qf_ref, kf_ref, vf_ref, bgf_ref, bgtf_ref, of_ref),
             (qb_ref, kb_ref, vb_ref, bgb_ref, bgtb_ref, ob_ref))):
        reverse = di == 1
        tri = jnp.where((r <= s) if reverse else (r >= s), 1.0, 0.0)
        bg = bg_ref[...]
        gc_cols = jnp.dot(tri, bg, precision=hi, preferred_element_type=F32)
        gc_rows = lax.dot_general(bgt_ref[...], tri, (((1,), (1,)), ((), ())),
                                  precision=hi, preferred_element_type=F32)
        for h in range(N_HEADS):
            sl = slice(h * HEAD_DIM, (h + 1) * HEAD_DIM)
            jb = di * N_HEADS + h
            jg = DN_BG // 2 + jb
            gc_col = gc_cols[:, jg:jg + 1]
            g_tot = gc_col[0:1, :] if reverse else gc_col[c - 1:c, :]
            o, st = _delta_chunk(q_ref[:, sl], k_ref[:, sl], v_ref[:, sl], bg[:, jb:jb + 1], gc_col,
                                 gc_rows[jg:jg + 1, :], g_tot, state_ref[jb], reverse)
            state_ref[jb] = st
            o_ref[:, sl] = o


def delta_rule(q, k, v, bg, bgt, chunk=DN_CHUNK):
    b, s, d = q.shape
    nc = s // chunk
    fwd = lambda bi, c: (bi, c, 0)
    bwd = lambda bi, c: (bi, nc - 1 - c, 0)
    fwd_t = lambda bi, c: (bi, 0, c)
    bwd_t = lambda bi, c: (bi, 0, nc - 1 - c)
    big = lambda im: pl.BlockSpec((None, chunk, d), im)
    return pl.pallas_call(
        functools.partial(_delta_kernel, chunk=chunk),
        grid=(b, nc),
        in_specs=[big(fwd), big(fwd), big(fwd), pl.BlockSpec((None, chunk, DN_BG), fwd),
                  pl.BlockSpec((None, DN_BG, chunk), fwd_t),
                  big(bwd), big(bwd), big(bwd), pl.BlockSpec((None, chunk, DN_BG), bwd),
                  pl.BlockSpec((None, DN_BG, chunk), bwd_t)],
        out_specs=[big(fwd), big(bwd)],
        out_shape=[jax.ShapeDtypeStruct((b, s, d), F32)] * 2,
        scratch_shapes=[pltpu.VMEM((2 * N_HEADS, HEAD_DIM, HEAD_DIM), F32)],
        compiler_params=_cparams("parallel", "arbitrary"),
        name="delta_rule",
    )(q, k, v, bg, bgt, q, k, v, bg, bgt)


def _dn_out_kernel(of_ref, ob_ref, z_ref, ng_ref, wo_ref, x_ref, o_ref):
    parts = []
    for h in range(N_HEADS):
        sl = slice(h * HEAD_DIM, (h + 1) * HEAD_DIM)
        o = of_ref[:, sl] + ob_ref[:, sl]
        o = o * lax.rsqrt(jnp.mean(o * o, axis=-1, keepdims=True) + RMS_EPS) * ng_ref[...] * _silu(z_ref[:, sl])
        parts.append(o.astype(BF16))
    o_ref[...] = x_ref[...] + _dot(jnp.concatenate(parts, axis=1), wo_ref[...])


def dn_out(o_f, o_b, proj, norm_g, w_out, x, tm=512):
    n, d = x.shape
    tile = lambda i: (i, 0)
    return pl.pallas_call(
        _dn_out_kernel,
        grid=(n // tm,),
        in_specs=[pl.BlockSpec((tm, d), tile), pl.BlockSpec((tm, d), tile),
                  pl.BlockSpec((tm, d), lambda i: (i, 3)),
                  pl.BlockSpec((1, HEAD_DIM), lambda i: (0, 0)),
                  pl.BlockSpec((d, d), lambda i: (0, 0)),
                  pl.BlockSpec((tm, d), tile)],
        out_specs=pl.BlockSpec((tm, d), tile),
        out_shape=jax.ShapeDtypeStruct((n, d), F32),
        compiler_params=_cparams("parallel"),
        name="dn_out",
    )(o_f, o_b, proj, norm_g.reshape(1, HEAD_DIM), w_out, x)


def _attn_kernel(q_ref, kp_ref, kc_ref, kn_ref, vp_ref, vc_ref, vn_ref, o_ref, lse_ref, *, tq, dilation, sub_len):
    i = pl.program_id(2)
    half = ATTN_HALF
    tk = tq + 2 * half
    row = lax.broadcasted_iota(I32, (tq, tk), 0)
    col = lax.broadcasted_iota(I32, (tq, tk), 1)
    rel = col - half - row
    key = i * tq - half + col
    valid = (jnp.abs(rel) <= half) & (key >= 0) & (key < sub_len)
    absrel = jnp.abs(rel).astype(F32) * float(dilation)
    lane = lax.broadcasted_iota(I32, (tq, LANES), 1)
    lse_tile = jnp.zeros((tq, LANES), F32)
    scale = HEAD_DIM ** -0.5
    for h in range(N_HEADS):
        sl = slice(h * HEAD_DIM, (h + 1) * HEAD_DIM)
        slope = 2.0 ** (-8.0 * (h + 1) / N_HEADS)
        kcat = jnp.concatenate([kp_ref[:, sl], kc_ref[:, sl], kn_ref[:, sl]], axis=0)
        vcat = jnp.concatenate([vp_ref[:, sl], vc_ref[:, sl], vn_ref[:, sl]], axis=0)
        sc = _dot_nt(q_ref[:, sl], kcat) * scale
        sc = jnp.where(valid, sc - slope * absrel, NEG_INF)
        m = jnp.max(sc, axis=-1, keepdims=True)
        p = jnp.exp(sc - m)
        den = jnp.sum(p, axis=-1, keepdims=True)
        o_ref[:, sl] = _dot(p.astype(BF16), vcat) / den
        lse_tile = jnp.where(lane == h, m + jnp.log(den), lse_tile)
    lse_ref[...] = lse_tile


def dilated_attention(proj, gi, dilation, tq=ATTN_TQ):
    b, s, wtot = proj.shape
    d = D_MODEL
    half = ATTN_HALF
    assert s % (dilation * half) == 0
    sub_len = s // dilation
    tq = min(tq, sub_len)
    nq = sub_len // tq
    hb = tq // half
    nhalo = sub_len // half
    ncol = wtot // d
    pr = proj.reshape(b, sub_len, dilation * wtot)
    base = gi * 3

    def cur(which):
        return pl.BlockSpec((None, tq, d), lambda bi, r, i: (bi, i, r * ncol + base + which))

    def prev(which):
        return pl.BlockSpec((None, half, d), lambda bi, r, i: (bi, jnp.maximum(i * hb - 1, 0), r * ncol + base + which))

    def nxt(which):
        return pl.BlockSpec((None, half, d),
                            lambda bi, r, i: (bi, jnp.minimum((i + 1) * hb, nhalo - 1), r * ncol + base + which))

    o, lse = pl.pallas_call(
        functools.partial(_attn_kernel, tq=tq, dilation=dilation, sub_len=sub_len),
        grid=(b, dilation, nq),
        in_specs=[cur(0), prev(1), cur(1), nxt(1), prev(2), cur(2), nxt(2)],
        out_specs=[pl.BlockSpec((None, tq, d), lambda bi, r, i: (bi, i, r)),
                   pl.BlockSpec((None, tq, LANES), lambda bi, r, i: (bi, i, r))],
        out_shape=[jax.ShapeDtypeStruct((b, sub_len, dilation * d), F32),
                   jax.ShapeDtypeStruct((b, sub_len, dilation * LANES), F32)],
        compiler_params=_cparams("parallel", "parallel", "parallel"),
        name=f"dilated_attn_g{gi}",
    )(pr, pr, pr, pr, pr, pr, pr)
    return o.reshape(b, s, d), lse.reshape(b, s, LANES)


def _attn_out_kernel(o0_ref, o1_ref, o2_ref, l0_ref, l1_ref, l2_ref, wo_ref, x_ref, out_ref):
    parts = []
    for h in range(N_HEADS):
        sl = slice(h * HEAD_DIM, (h + 1) * HEAD_DIM)
        a0, a1, a2 = l0_ref[:, h:h + 1], l1_ref[:, h:h + 1], l2_ref[:, h:h + 1]
        mx = jnp.maximum(jnp.maximum(a0, a1), a2)
        e0, e1, e2 = jnp.exp(a0 - mx), jnp.exp(a1 - mx), jnp.exp(a2 - mx)
        o = (e0 * o0_ref[:, sl] + e1 * o1_ref[:, sl] + e2 * o2_ref[:, sl]) / (e0 + e1 + e2)
        parts.append(o.astype(BF16))
    out_ref[...] = x_ref[...] + _dot(jnp.concatenate(parts, axis=1), wo_ref[...])


def attn_out(os, lses, w_out, x, tm=512):
    n, d = x.shape
    tile = lambda i: (i, 0)
    big = pl.BlockSpec((tm, d), tile)
    small = pl.BlockSpec((tm, LANES), tile)
    return pl.pallas_call(
        _attn_out_kernel,
        grid=(n // tm,),
        in_specs=[big, big, big, small, small, small, pl.BlockSpec((d, d), lambda i: (0, 0)), big],
        out_specs=big,
        out_shape=jax.ShapeDtypeStruct((n, d), F32),
        compiler_params=_cparams("parallel"),
        name="attn_out",
    )(*os, *lses, w_out, x)


def _route_kernel(x_ref, g_ref, r_ref, h_ref, aff_ref):
    h = _rms(x_ref[...], g_ref[...])
    h_ref[...] = h.astype(BF16)
    logits = jnp.dot(h, r_ref[...], precision=lax.Precision.HIGHEST, preferred_element_type=F32)
    lt = logits.T[:N_EXPERTS, :]
    m = jnp.max(lt, axis=0, keepdims=True)
    e = jnp.exp(lt - m)
    aff_ref[...] = e / jnp.sum(e, axis=0, keepdims=True)


def moe_route(x, g, router, tm=512):
    n, d = x.shape
    rpad = jnp.zeros((d, LANES), F32).at[:, :N_EXPERTS].set(router)
    return pl.pallas_call(
        _route_kernel,
        grid=(n // tm,),
        in_specs=[pl.BlockSpec((tm, d), lambda i: (i, 0)),
                  pl.BlockSpec((1, d), lambda i: (0, 0)),
                  pl.BlockSpec((d, LANES), lambda i: (0, 0))],
        out_specs=[pl.BlockSpec((tm, d), lambda i: (i, 0)),
                   pl.BlockSpec((N_EXPERTS, tm), lambda i: (0, i))],
        out_shape=[jax.ShapeDtypeStruct((n, d), BF16), jax.ShapeDtypeStruct((N_EXPERTS, n), F32)],
        compiler_params=_cparams("parallel"),
        name="moe_route",
    )(x, g.reshape(1, d), rpad)


def _select_kernel(aff_ref, sel_ref, pos_ref, *, cap):
    e, n = aff_ref.shape
    bits = pltpu.bitcast(aff_ref[...], I32)
    tidx = lax.broadcasted_iota(I32, (e, n), 1)

    def count(mask_i32):
        return jnp.sum(mask_i32.astype(F32), axis=1, keepdims=True).astype(I32)

    def thr_body(it, thr):
        cand = thr | jnp.left_shift(jnp.int32(1), 30 - it)
        cnt = count(jnp.where(bits >= cand, 1, 0))
        return jnp.where(cnt >= cap, cand, thr)

    thr = lax.fori_loop(0, 31, thr_body, jnp.zeros((e, 1), I32))
    gt = jnp.where(bits > thr, 1, 0)
    eq = jnp.where(bits == thr, 1, 0)
    need = cap - count(gt)
    nbits = max(1, (n - 1).bit_length())

    def tie_body(it, j):
        cand = j | jnp.left_shift(jnp.int32(1), nbits - 1 - it)
        cnt = count(jnp.where(tidx < cand, eq, 0))
        return jnp.where(cnt < need, cand, j)

    jlast = lax.fori_loop(0, nbits, tie_body, jnp.zeros((e, 1), I32))
    sel = gt + jnp.where(tidx <= jlast, eq, 0)
    sel_ref[...] = sel
    rr = lax.broadcasted_iota(I32, (LANES, LANES), 0)
    cc = lax.broadcasted_iota(I32, (LANES, LANES), 1)
    upper = jnp.where(rr <= cc, 1.0, 0.0).astype(BF16)

    def blk_body(bi, run):
        off = pl.multiple_of(bi * LANES, LANES)
        sb = sel_ref[:, pl.ds(off, LANES)]
        inc = _dot(sb.astype(F32).astype(BF16), upper).astype(I32)
        pos_ref[:, pl.ds(off, LANES)] = inc - sb + run
        return run + inc[:, LANES - 1:LANES]

    lax.fori_loop(0, n // LANES, blk_body, jnp.zeros((e, 1), I32))


def moe_select(aff_t, cap):
    e, n = aff_t.shape
    return pl.pallas_call(
        functools.partial(_select_kernel, cap=cap),
        grid=(1,),
        in_specs=[pl.BlockSpec((e, n), lambda i: (0, 0))],
        out_specs=[pl.BlockSpec((e, n), lambda i: (0, 0)), pl.BlockSpec((e, n), lambda i: (0, 0))],
        out_shape=[jax.ShapeDtypeStruct((e, n), I32), jax.ShapeDtypeStruct((e, n), I32)],
        compiler_params=_cparams("arbitrary"),
        name="moe_select",
    )(aff_t)


def _ffn_kernel(x_ref, gate_ref, wg_ref, wu_ref, wd_ref, o_ref):
    x = x_ref[...]
    acc = None
    for f in range(EXPERT_FF // FF_CHUNK):
        fs = slice(f * FF_CHUNK, (f + 1) * FF_CHUNK)
        hid = (_silu(_dot(x, wg_ref[:, fs])) * _dot(x, wu_ref[:, fs])).astype(BF16)
        part = _dot(hid, wd_ref[fs, :])
        acc = part if acc is None else acc + part
    o_ref[...] = (acc * gate_ref[...]).astype(o_ref.dtype)


def moe_ffn(xe, gate, w_gate, w_up, w_down, tm=512):
    e, c, d = xe.shape
    f = w_gate.shape[2]
    return pl.pallas_call(
        _ffn_kernel,
        grid=(e, c // tm),
        in_specs=[pl.BlockSpec((None, tm, d), lambda ei, j: (ei, j, 0)),
                  pl.BlockSpec((None, tm, 1), lambda ei, j: (ei, j, 0)),
                  pl.BlockSpec((None, d, f), lambda ei, j: (ei, 0, 0)),
                  pl.BlockSpec((None, d, f), lambda ei, j: (ei, 0, 0)),
                  pl.BlockSpec((None, f, d), lambda ei, j: (ei, 0, 0))],
        out_specs=pl.BlockSpec((None, tm, d), lambda ei, j: (ei, j, 0)),
        out_shape=jax.ShapeDtypeStruct((e, c, d), BF16),
        compiler_params=_cparams("parallel", "arbitrary"),
        name="moe_ffn",
    )(xe, gate, w_gate, w_up, w_down)


def _combine_dmas(tab_ref, ye_ref, zbuf_ref, sem, i, cap, total, tile):
    copies, bases = [], []
    off = jnp.int32(0)
    max_shift = (tile // BF16_SUBLANES).bit_length()
    for e in range(N_EXPERTS):
        p0 = tab_ref[i * N_EXPERTS + e]
        cnt = tab_ref[(i + 1) * N_EXPERTS + e] - p0
        g0 = e * cap + p0
        a = (g0 // BF16_SUBLANES) * BF16_SUBLANES
        nrow = jnp.where(cnt > 0, ((g0 - a + cnt + BF16_SUBLANES - 1) // BF16_SUBLANES) * BF16_SUBLANES, 0)
        a = jnp.minimum(a, total - nrow)
        bases.append(off - a + e * cap)
        src, dst = a, off
        for sh in range(max_shift, -1, -1):
            size = BF16_SUBLANES << sh
            take = (nrow & size) != 0
            cp = pltpu.make_async_copy(ye_ref.at[pl.ds(pl.multiple_of(src, BF16_SUBLANES), size)],
                                       zbuf_ref.at[pl.ds(pl.multiple_of(dst, BF16_SUBLANES), size)], sem)
            copies.append((take, cp))
            step = jnp.where(take, size, 0)
            src, dst = src + step, dst + step
        off = off + nrow
    return copies, bases, off


def _combine_kernel(tab_ref, x_ref, sel_ref, pos_ref, ye_ref, g_ref, o_ref, zbuf_ref, sem,
                    *, cap, total, tile, final_norm):
    i = pl.program_id(0)

    @pl.when(i == 0)
    def _():
        zbuf_ref[...] = jnp.zeros_like(zbuf_ref)

    copies, bases, k_tot = _combine_dmas(tab_ref, ye_ref, zbuf_ref, sem, i, cap, total, tile)
    for take, cp in copies:
        @pl.when(take)
        def _():
            cp.start()

    lane_e = lax.broadcasted_iota(I32, (1, N_EXPERTS), 1)
    cvec = jnp.zeros((1, N_EXPERTS), I32)
    for e in range(N_EXPERTS):
        cvec = jnp.where(lane_e == e, bases[e], cvec)
    col = jnp.where(sel_ref[...] > 0, pos_ref[...] + cvec, -1)
    o_ref[...] = x_ref[...]

    for take, cp in copies:
        @pl.when(take)
        def _():
            cp.wait()

    kc = COMBINE_KCHUNK

    def chunk_body(ci, carry):
        base = pl.multiple_of(ci * kc, kc)
        z = zbuf_ref[pl.ds(base, kc), :]
        lanes = lax.broadcasted_iota(I32, (tile, kc), 1) + base
        p = jnp.zeros((tile, kc), F32)
        for e in range(N_EXPERTS):
            p = jnp.where(col[:, e:e + 1] == lanes, 1.0, p)
        o_ref[...] += _dot(p.astype(BF16), z)
        return carry

    lax.fori_loop(0, (k_tot + kc - 1) // kc, chunk_body, 0)
    if final_norm:
        o_ref[...] = _rms(o_ref[...], g_ref[...])


def moe_combine(x, sel_t, pos_t, tab, ye, cap, norm_g=None, tile=COMBINE_TILE):
    n, d = x.shape
    total = ye.shape[0]
    kmax = N_EXPERTS * (tile + BF16_SUBLANES)
    kmax = -(-kmax // COMBINE_KCHUNK) * COMBINE_KCHUNK
    final_norm = norm_g is not None
    g = (norm_g if final_norm else jnp.ones((d,), F32)).reshape(1, d)
    grid_spec = pltpu.PrefetchScalarGridSpec(
        num_scalar_prefetch=1,
        grid=(n // tile,),
        in_specs=[pl.BlockSpec((tile, d), lambda i, tab: (i, 0)),
                  pl.BlockSpec((tile, N_EXPERTS), lambda i, tab: (i, 0)),
                  pl.BlockSpec((tile, N_EXPERTS), lambda i, tab: (i, 0)),
                  pl.BlockSpec(memory_space=pl.ANY),
                  pl.BlockSpec((1, d), lambda i, tab: (0, 0))],
        out_specs=pl.BlockSpec((tile, d), lambda i, tab: (i, 0)),
        scratch_shapes=[pltpu.VMEM((kmax, d), BF16), pltpu.SemaphoreType.DMA],
    )
    return pl.pallas_call(
        functools.partial(_combine_kernel, cap=cap, total=total, tile=tile, final_norm=final_norm),
        grid_spec=grid_spec,
        out_shape=jax.ShapeDtypeStruct((n, d), F32),
        compiler_params=_cparams("arbitrary"),
        name="moe_combine",
    )(tab, x, sel_t, pos_t, ye, g)


def expert_choice_moe(x, norm_g, router, w_gate, w_up, w_down, final_g=None):
    n, d = x.shape
    cap = max(1, CAPACITY_FACTOR * n // N_EXPERTS)
    hb, aff_t = moe_route(x, norm_g, router)
    sel, pos = moe_select(aff_t, cap)
    tok = jnp.arange(n, dtype=I32)[None, :]
    idx = jnp.sort(jnp.where(sel > 0, tok, tok + n), axis=1)[:, :cap]
    gate = jnp.take_along_axis(aff_t, idx, axis=1)[..., None]
    xe = jnp.take(hb, idx.reshape(-1), axis=0).reshape(N_EXPERTS, cap, d)
    ye = moe_ffn(xe, gate, w_gate, w_up, w_down).reshape(N_EXPERTS * cap, d)
    tab = jnp.concatenate([pos[:, ::COMBINE_TILE].T, jnp.full((1, N_EXPERTS), cap, I32)], axis=0).reshape(-1)
    return moe_combine(x, sel.T, pos.T, tab, ye, cap, norm_g=final_g)


def _trunk(x, p):
    b, s, d = x.shape
    n = b * s
    x = x.reshape(n, d)
    for i in range(DEPTH):
        kind, j = i % N_MIXERS, i // N_MIXERS
        if kind == 0:
            u = norm_glu(x, p["norm_mix"][i], p["conv_w_glu"][j], p["conv_b_glu"][j])
            x = conv_out(u.reshape(b, s, d), x.reshape(b, s, d), p["conv_w_dw"][j], p["conv_b_dw"][j],
                         p["conv_ln_g"][j], p["conv_ln_b"][j], p["conv_w_out"][j], p["conv_b_out"][j]).reshape(n, d)
        elif kind == 1:
            proj = norm_linear(x, p["norm_mix"][i], p["dn_w_in"][j], F32)
            q, k, v, bg, bgt = dn_prep(proj.reshape(b, s, -1), p["dn_w_conv"][j], p["dn_a_log"][j], p["dn_dt_bias"][j])
            o_f, o_b = delta_rule(q, k, v, bg, bgt)
            x = dn_out(o_f.reshape(n, d), o_b.reshape(n, d), proj, p["dn_norm_g"][j], p["dn_w_out"][j], x)
        else:
            proj = norm_linear(x, p["norm_mix"][i], p["attn_w_in"][j], BF16).reshape(b, s, -1)
            os, lses = [], []
            for gi, (window, dilation) in enumerate(ATTN_GROUPS):
                assert window // (2 * dilation) == ATTN_HALF
                o, lse = dilated_attention(proj, gi, dilation)
                os.append(o.reshape(n, d))
                lses.append(lse.reshape(n, LANES))
            x = attn_out(os, lses, p["attn_w_out"][j], x)
        x = expert_choice_moe(x, p["norm_ffn"][i], p["moe_router"][i], p["moe_w_gate"][i], p["moe_w_up"][i],
                              p["moe_w_down"][i], final_g=p["norm_final"] if i == DEPTH - 1 else None)
    return x.reshape(b, s, d)


def kernel(x_prompt, x_sample, norm_mix, norm_ffn, norm_final, conv_w_glu, conv_b_glu, conv_w_dw, conv_b_dw,
           conv_ln_g, conv_ln_b, conv_w_out, conv_b_out, dn_w_in, dn_w_conv, dn_a_log, dn_dt_bias, dn_norm_g,
           dn_w_out, attn_w_in, attn_w_out, moe_router, moe_w_gate, moe_w_up, moe_w_down):
    dn_pad = (-dn_w_in.shape[-1]) % 512
    p = dict(
        norm_mix=norm_mix, norm_ffn=norm_ffn, norm_final=norm_final,
        conv_w_glu=conv_w_glu.astype(BF16), conv_b_glu=conv_b_glu, conv_w_dw=conv_w_dw, conv_b_dw=conv_b_dw,
        conv_ln_g=conv_ln_g, conv_ln_b=conv_ln_b, conv_w_out=conv_w_out.astype(BF16), conv_b_out=conv_b_out,
        dn_w_in=jnp.pad(dn_w_in, ((0, 0), (0, 0), (0, dn_pad))).astype(BF16), dn_w_conv=dn_w_conv,
        dn_a_log=dn_a_log, dn_dt_bias=dn_dt_bias, dn_norm_g=dn_norm_g, dn_w_out=dn_w_out.astype(BF16),
        attn_w_in=attn_w_in.astype(BF16), attn_w_out=attn_w_out.astype(BF16),
        moe_router=moe_router, moe_w_gate=moe_w_gate.astype(BF16), moe_w_up=moe_w_up.astype(BF16),
        moe_w_down=moe_w_down.astype(BF16),
    )
    return (_trunk(x_prompt, p), _trunk(x_sample, p))
```

```python
import functools
import math

import jax
import jax.numpy as jnp
from jax import lax
from jax.experimental import pallas as pl
from jax.experimental.pallas import tpu as pltpu

F32 = jnp.float32
BF16 = jnp.bfloat16
I32 = jnp.int32

D_MODEL = 1024
DEPTH = 4
N_MIXERS = 3
RMS_EPS = 1e-6
LN_EPS = 1e-5
NEG_INF = -1e30
CONV_WIDTH = 31
HEAD_DIM = 128
N_HEADS = D_MODEL // HEAD_DIM
SHORT_CONV = 5
ATTN_GROUPS = ((128, 1), (512, 4), (2048, 16))
N_EXPERTS = 16
EXPERT_FF = 2 * D_MODEL
CAPACITY_FACTOR = 2

VMEM_LIMIT_BYTES = 56 * 1024 * 1024
LANES = 128
BF16_SUBLANES = 16

DN_CHUNK = 64
DN_SPLIT_PASSES = False
ATTN_HALF = 64
ATTN_TQ = 128
ATTN_PERM = 16
ATTN_TP = ATTN_PERM * ATTN_HALF
ATTN_RES_PER_STEP = 4
FF_CHUNK = 512
COMBINE_TILE = 256
COMBINE_KCHUNK = 256


def _cparams(*sem):
    return pltpu.CompilerParams(dimension_semantics=sem, vmem_limit_bytes=VMEM_LIMIT_BYTES)


def _rms(x, g):
    return x * lax.rsqrt(jnp.mean(x * x, axis=-1, keepdims=True) + RMS_EPS) * g


def _silu(x):
    return x * jax.nn.sigmoid(x)


def _dot(a, b):
    return jnp.dot(a, b, preferred_element_type=F32)


def _dot_nt(a, b):
    return lax.dot_general(a, b, (((1,), (1,)), ((), ())), preferred_element_type=F32)


def _dot_tn(a, b):
    return lax.dot_general(a, b, (((0,), (0,)), ((), ())), preferred_element_type=F32)


def _norm_linear_kernel(x_ref, g_ref, w_ref, o_ref, h_ref):
    @pl.when(pl.program_id(1) == 0)
    def _():
        h_ref[...] = _rms(x_ref[...], g_ref[...]).astype(BF16)

    o_ref[...] = _dot(h_ref[...], w_ref[...]).astype(o_ref.dtype)


def norm_linear(x, g, w, out_dtype, tm=1024, tn=512):
    n, d = x.shape
    nout = w.shape[1]
    return pl.pallas_call(
        _norm_linear_kernel,
        grid=(n // tm, nout // tn),
        in_specs=[pl.BlockSpec((tm, d), lambda i, j: (i, 0)),
                  pl.BlockSpec((1, d), lambda i, j: (0, 0)),
                  pl.BlockSpec((d, tn), lambda i, j: (0, j))],
        out_specs=pl.BlockSpec((tm, tn), lambda i, j: (i, j)),
        out_shape=jax.ShapeDtypeStruct((n, nout), out_dtype),
        scratch_shapes=[pltpu.VMEM((tm, d), BF16)],
        compiler_params=_cparams("parallel", "arbitrary"),
        name="norm_linear",
    )(x, g.reshape(1, d), w)


def _norm_glu_kernel(x_ref, g_ref, wa_ref, wg_ref, ba_ref, bg_ref, o_ref, h_ref):
    @pl.when(pl.program_id(1) == 0)
    def _():
        h_ref[...] = _rms(x_ref[...], g_ref[...]).astype(BF16)

    h = h_ref[...]
    a = _dot(h, wa_ref[...]) + ba_ref[...]
    gate = _dot(h, wg_ref[...]) + bg_ref[...]
    o_ref[...] = a * jax.nn.sigmoid(gate)


def norm_glu(x, g, w, b, tm=1024, tn=512):
    n, d = x.shape
    nb = d // tn
    b2 = b.reshape(1, 2 * d)
    return pl.pallas_call(
        _norm_glu_kernel,
        grid=(n // tm, nb),
        in_specs=[pl.BlockSpec((tm, d), lambda i, j: (i, 0)),
                  pl.BlockSpec((1, d), lambda i, j: (0, 0)),
                  pl.BlockSpec((d, tn), lambda i, j: (0, j)),
                  pl.BlockSpec((d, tn), lambda i, j: (0, j + nb)),
                  pl.BlockSpec((1, tn), lambda i, j: (0, j)),
                  pl.BlockSpec((1, tn), lambda i, j: (0, j + nb))],
        out_specs=pl.BlockSpec((tm, tn), lambda i, j: (i, j)),
        out_shape=jax.ShapeDtypeStruct((n, d), F32),
        scratch_shapes=[pltpu.VMEM((tm, d), BF16)],
        compiler_params=_cparams("parallel", "arbitrary"),
        name="norm_glu",
    )(x, g.reshape(1, d), w, w, b2, b2)


CONV_HALO = 16


def _conv_out_kernel(up_ref, uc_ref, un_ref, wdw_ref, bdw_ref, lng_ref, lnb_ref, wo_ref, bo_ref, x_ref,
                     o_ref, win_ref, *, ts):
    i = pl.program_id(1)
    last = pl.num_programs(1) - 1
    win_ref[0:CONV_HALO, :] = jnp.where(i > 0, up_ref[...], 0.0)
    win_ref[CONV_HALO:CONV_HALO + ts, :] = uc_ref[...]
    win_ref[CONV_HALO + ts:, :] = jnp.where(i < last, un_ref[...], 0.0)
    acc = jnp.zeros((ts, D_MODEL), F32) + bdw_ref[...]
    for s in range(8):
        ws = win_ref[pl.ds(s, ts + 24), :]
        for a in range(4):
            j = 8 * a + s - 1
            if 0 <= j < CONV_WIDTH:
                acc = acc + wdw_ref[j:j + 1, :] * ws[8 * a:8 * a + ts, :]
    mu = jnp.mean(acc, axis=-1, keepdims=True)
    cen = acc - mu
    var = jnp.mean(cen * cen, axis=-1, keepdims=True)
    u = cen * lax.rsqrt(var + LN_EPS) * lng_ref[...] + lnb_ref[...]
    y = _silu(u).astype(BF16)
    o_ref[...] = x_ref[...] + _dot(y, wo_ref[...]) + bo_ref[...]


def conv_out(u, x, w_dw, b_dw, ln_g, ln_b, w_out, b_out, ts=256):
    b, s, d = u.shape
    hb = ts // CONV_HALO
    nblk = s // CONV_HALO
    wdw = jnp.concatenate([w_dw, jnp.zeros((1, d), F32)], axis=0)
    row = lambda v: v.reshape(1, d)
    return pl.pallas_call(
        functools.partial(_conv_out_kernel, ts=ts),
        grid=(b, s // ts),
        in_specs=[pl.BlockSpec((None, CONV_HALO, d), lambda bi, i: (bi, jnp.maximum(i * hb - 1, 0), 0)),
                  pl.BlockSpec((None, ts, d), lambda bi, i: (bi, i, 0)),
                  pl.BlockSpec((None, CONV_HALO, d), lambda bi, i: (bi, jnp.minimum((i + 1) * hb, nblk - 1), 0)),
                  pl.BlockSpec((CONV_WIDTH + 1, d), lambda bi, i: (0, 0)),
                  pl.BlockSpec((1, d), lambda bi, i: (0, 0)),
                  pl.BlockSpec((1, d), lambda bi, i: (0, 0)),
                  pl.BlockSpec((1, d), lambda bi, i: (0, 0)),
                  pl.BlockSpec((d, d), lambda bi, i: (0, 0)),
                  pl.BlockSpec((1, d), lambda bi, i: (0, 0)),
                  pl.BlockSpec((None, ts, d), lambda bi, i: (bi, i, 0))],
        out_specs=pl.BlockSpec((None, ts, d), lambda bi, i: (bi, i, 0)),
        out_shape=jax.ShapeDtypeStruct((b, s, d), F32),
        scratch_shapes=[pltpu.VMEM((ts + 2 * CONV_HALO, d), F32)],
        compiler_params=_cparams("parallel", "parallel"),
        name="conv_out",
    )(u, u, u, wdw, row(b_dw), row(ln_g), row(ln_b), w_out, row(b_out), x)


DN_HALO = 8
DN_BG = 32


def _dn_prep_kernel(pp_ref, pc_ref, pn_ref, ba_ref, wc_ref, alog_ref, dtb_ref,
                    q_ref, k_ref, v_ref, bg_ref, win_ref, *, ts):
    i = pl.program_id(1)
    last = pl.num_programs(1) - 1
    w3 = 3 * D_MODEL
    win_ref[0:DN_HALO, :] = jnp.where(i > 0, pp_ref[...], 0.0)
    win_ref[DN_HALO:DN_HALO + ts, :] = pc_ref[...]
    win_ref[DN_HALO + ts:, :] = jnp.where(i < last, pn_ref[...], 0.0)
    left = (SHORT_CONV - 1) // 2
    acc = jnp.zeros((ts, w3), F32)
    for j in range(SHORT_CONV):
        acc = acc + wc_ref[j:j + 1, :] * win_ref[pl.ds(DN_HALO - left + j, ts), :]
    qkv = _silu(acc)

    def l2n(t):
        return t * lax.rsqrt(jnp.sum(t * t, axis=-1, keepdims=True) + RMS_EPS)

    for h in range(N_HEADS):
        sl = slice(h * HEAD_DIM, (h + 1) * HEAD_DIM)
        q_ref[:, sl] = l2n(qkv[:, sl]) * (HEAD_DIM ** -0.5)
        k_ref[:, sl] = l2n(qkv[:, D_MODEL + h * HEAD_DIM:D_MODEL + (h + 1) * HEAD_DIM])
    v_ref[...] = qkv[:, 2 * D_MODEL:]
    ba = ba_ref[:, :DN_BG]
    lane = lax.broadcasted_iota(I32, (ts, DN_BG), 1)
    beta = jax.nn.sigmoid(ba)
    g = -jnp.exp(alog_ref[...]) * jax.nn.softplus(ba + dtb_ref[...])
    bg_ref[...] = jnp.where(lane < DN_BG // 2, beta, g)


def dn_prep(proj, w_conv, a_log, dt_bias, ts=256):
    b, s, _ = proj.shape
    d = D_MODEL
    hb = ts // DN_HALO
    nblk = s // DN_HALO
    zeros16 = jnp.zeros((16,), F32)
    alog = jnp.concatenate([zeros16, a_log.reshape(-1)]).reshape(1, DN_BG)
    dtb = jnp.concatenate([zeros16, dt_bias.reshape(-1)]).reshape(1, DN_BG)
    bacol = 4 * d // LANES
    tile = lambda bi, i: (bi, i, 0)
    return pl.pallas_call(
        functools.partial(_dn_prep_kernel, ts=ts),
        grid=(b, s // ts),
        in_specs=[pl.BlockSpec((None, DN_HALO, 3 * d), lambda bi, i: (bi, jnp.maximum(i * hb - 1, 0), 0)),
                  pl.BlockSpec((None, ts, 3 * d), tile),
                  pl.BlockSpec((None, DN_HALO, 3 * d), lambda bi, i: (bi, jnp.minimum((i + 1) * hb, nblk - 1), 0)),
                  pl.BlockSpec((None, ts, LANES), lambda bi, i: (bi, i, bacol)),
                  pl.BlockSpec((SHORT_CONV, 3 * d), lambda bi, i: (0, 0)),
                  pl.BlockSpec((1, DN_BG), lambda bi, i: (0, 0)),
                  pl.BlockSpec((1, DN_BG), lambda bi, i: (0, 0))],
        out_specs=[pl.BlockSpec((None, ts, d), tile),
                   pl.BlockSpec((None, ts, d), tile),
                   pl.BlockSpec((None, ts, d), tile),
                   pl.BlockSpec((None, ts, DN_BG), tile)],
        out_shape=[jax.ShapeDtypeStruct((b, s, d), F32)] * 3 + [jax.ShapeDtypeStruct((b, s, DN_BG), F32)],
        scratch_shapes=[pltpu.VMEM((ts + 2 * DN_HALO, 3 * d), F32)],
        compiler_params=_cparams("parallel", "parallel"),
        name="dn_prep",
    )(proj, proj, proj, proj, w_conv, alog, dtb)


def _delta_kernel(qf_ref, kf_ref, vf_ref, bgf_ref, qb_ref, kb_ref, vb_ref, bgb_ref,
                  of_ref, ob_ref, state_ref, *, chunk):
    @pl.when(pl.program_id(1) == 0)
    def _():
        state_ref[...] = jnp.zeros_like(state_ref)

    c = chunk
    dh = HEAD_DIM
    ns = 2 * dh
    r = lax.broadcasted_iota(I32, (c, c), 0)
    s = lax.broadcasted_iota(I32, (c, c), 1)
    eye = jnp.where(lax.broadcasted_iota(I32, (DN_BG, DN_BG), 0) == lax.broadcasted_iota(I32, (DN_BG, DN_BG), 1),
                    1.0, 0.0)
    hi = lax.Precision.HIGHEST
    chains = []
    for di, (q_ref, k_ref, v_ref, bg_ref, o_ref) in enumerate(
            ((qf_ref, kf_ref, vf_ref, bgf_ref, of_ref), (qb_ref, kb_ref, vb_ref, bgb_ref, ob_ref))):
        reverse = di == 1
        incl = (r <= s) if reverse else (r >= s)
        strict = (r < s) if reverse else (r > s)
        bg = bg_ref[...]
        gc_cols = jnp.dot(jnp.where(incl, 1.0, 0.0), bg, precision=hi, preferred_element_type=F32)
        gc_rows = lax.dot_general(eye, gc_cols, (((1,), (1,)), ((), ())),
                                  precision=hi, preferred_element_type=F32)
        for h in range(N_HEADS):
            jb = di * N_HEADS + h
            jg = DN_BG // 2 + jb
            gc_col = gc_cols[:, jg:jg + 1]
            chains.append(dict(
                sl=slice(h * dh, (h + 1) * dh), jb=jb, q_ref=q_ref, k_ref=k_ref, v_ref=v_ref, o_ref=o_ref,
                incl=incl, strict=strict, beta=bg[:, jb:jb + 1], gc_col=gc_col, gc_row=gc_rows[jg:jg + 1, :],
                g_tot=gc_col[0:1, :] if reverse else gc_col[c - 1:c, :]))

    for ch in chains:
        q, k, v = ch["q_ref"][:, ch["sl"]], ch["k_ref"][:, ch["sl"]], ch["v_ref"][:, ch["sl"]]
        decay = jnp.exp(jnp.where(ch["incl"], ch["gc_col"] - ch["gc_row"], NEG_INF))
        kb = k * ch["beta"]
        egc = jnp.exp(ch["gc_col"])
        qk_kk = _dot_nt(jnp.concatenate([q, kb], axis=0).astype(BF16), k.astype(BF16))
        ch["a"] = jnp.where(ch["incl"], qk_kk[:c] * decay, 0.0).astype(BF16)
        m = jnp.where(ch["strict"], -qk_kk[c:] * decay, 0.0)
        ch["x"] = jnp.concatenate([v * ch["beta"], kb * egc, m], axis=1)
        ch["qd"] = (q * egc).astype(BF16)
        ch["kd"] = (k * jnp.exp(ch["g_tot"] - ch["gc_col"])).astype(BF16)

    n_levels = max(1, (c - 1).bit_length())
    for lvl in range(n_levels):
        width = ns + c if lvl + 1 < n_levels else ns
        for ch in chains:
            x = ch["x"]
            xh = x.astype(BF16)
            if DN_SPLIT_PASSES:
                xl = (x - xh.astype(F32)).astype(BF16)
                lhs = jnp.concatenate([xh[:, ns:], xh[:, ns:], xl[:, ns:]], axis=1)
                rhs = jnp.concatenate([xh[:, :width], xl[:, :width], xh[:, :width]], axis=0)
            else:
                lhs, rhs = xh[:, ns:], xh[:, :width]
            prod = _dot(lhs, rhs)
            sol = x[:, :ns] + prod[:, :ns]
            ch["x"] = jnp.concatenate([sol, prod[:, ns:]], axis=1) if width > ns else sol

    for ch in chains:
        ch["state"] = state_ref[ch["jb"]]
        lhs = jnp.concatenate([ch["x"][:, dh:].astype(BF16), ch["qd"]], axis=0)
        ch["ws"] = _dot(lhs, ch["state"].astype(BF16))
    for ch in chains:
        ch["vb"] = (ch["x"][:, :dh] - ch["ws"][:c]).astype(BF16)
        ch["o_ref"][:, ch["sl"]] = ch["ws"][c:] + _dot(ch["a"], ch["vb"])
    for ch in chains:
        state_ref[ch["jb"]] = ch["state"] * jnp.exp(ch["g_tot"]) + _dot_tn(ch["kd"], ch["vb"])


def delta_rule(q, k, v, bg, chunk=DN_CHUNK):
    b, s, d = q.shape
    nc = s // chunk
    fwd = lambda bi, c: (bi, c, 0)
    bwd = lambda bi, c: (bi, nc - 1 - c, 0)
    big = lambda im: pl.BlockSpec((None, chunk, d), im)
    return pl.pallas_call(
        functools.partial(_delta_kernel, chunk=chunk),
        grid=(b, nc),
        in_specs=[big(fwd), big(fwd), big(fwd), pl.BlockSpec((None, chunk, DN_BG), fwd),
                  big(bwd), big(bwd), big(bwd), pl.BlockSpec((None, chunk, DN_BG), bwd)],
        out_specs=[big(fwd), big(bwd)],
        out_shape=[jax.ShapeDtypeStruct((b, s, d), F32)] * 2,
        scratch_shapes=[pltpu.VMEM((2 * N_HEADS, HEAD_DIM, HEAD_DIM), F32)],
        compiler_params=_cparams("parallel", "arbitrary"),
        name="delta_rule",
    )(q, k, v, bg, q, k, v, bg)


def _dn_out_kernel(of_ref, ob_ref, z_ref, ng_ref, wo_ref, x_ref, o_ref):
    parts = []
    for h in range(N_HEADS):
        sl = slice(h * HEAD_DIM, (h + 1) * HEAD_DIM)
        o = of_ref[:, sl] + ob_ref[:, sl]
        o = o * lax.rsqrt(jnp.mean(o * o, axis=-1, keepdims=True) + RMS_EPS) * ng_ref[...] * _silu(z_ref[:, sl])
        parts.append(o.astype(BF16))
    o_ref[...] = x_ref[...] + _dot(jnp.concatenate(parts, axis=1), wo_ref[...])


def dn_out(o_f, o_b, proj, norm_g, w_out, x, tm=512):
    n, d = x.shape
    tile = lambda i: (i, 0)
    return pl.pallas_call(
        _dn_out_kernel,
        grid=(n // tm,),
        in_specs=[pl.BlockSpec((tm, d), tile), pl.BlockSpec((tm, d), tile),
                  pl.BlockSpec((tm, d), lambda i: (i, 3)),
                  pl.BlockSpec((1, HEAD_DIM), lambda i: (0, 0)),
                  pl.BlockSpec((d, d), lambda i: (0, 0)),
                  pl.BlockSpec((tm, d), tile)],
        out_specs=pl.BlockSpec((tm, d), tile),
        out_shape=jax.ShapeDtypeStruct((n, d), F32),
        compiler_params=_cparams("parallel"),
        name="dn_out",
    )(o_f, o_b, proj, norm_g.reshape(1, HEAD_DIM), w_out, x)


def _attn_kernel(q_ref, kp_ref, kc_ref, kn_ref, vp_ref, vc_ref, vn_ref, o_ref, lse_ref, o_scr, l_scr,
                 *, dilation, w, g, sub_len):
    t = pl.program_id(1)
    half = ATTN_HALF
    tq = min(ATTN_TQ, w)
    nsub = w // tq
    tk = tq + 2 * half
    row = lax.broadcasted_iota(I32, (tq, tk), 0)
    col = lax.broadcasted_iota(I32, (tq, tk), 1)
    rel = col - half - row
    band = jnp.abs(rel) <= half
    absrel = jnp.abs(rel).astype(F32) * float(dilation)
    lane = lax.broadcasted_iota(I32, (tq, LANES), 1)
    scale = HEAD_DIM ** -0.5

    def window(p_ref, c_ref, n_ref, gg, j, sl):
        r0 = gg * w + j * tq
        hrow = slice(gg * half, (gg + 1) * half) if g > 1 else slice(None)
        before = p_ref[hrow, sl] if j == 0 else c_ref[r0 - half:r0, sl]
        after = n_ref[hrow, sl] if j == nsub - 1 else c_ref[r0 + tq:r0 + tq + half, sl]
        return jnp.concatenate([before, c_ref[r0:r0 + tq, sl], after], axis=0)

    blocks = [(gg, j) for gg in range(g) for j in range(nsub)]
    groups = [blocks] if g > 1 else [[blk] for blk in blocks]
    for group in groups:
        items = []
        for gg, j in group:
            r0 = gg * w + j * tq
            key = t * w + j * tq - half + col
            valid = band & (key >= 0) & (key < sub_len)
            for h in range(N_HEADS):
                sl = slice(h * HEAD_DIM, (h + 1) * HEAD_DIM)
                sc = _dot_nt(q_ref[r0:r0 + tq, sl], window(kp_ref, kc_ref, kn_ref, gg, j, sl))
                items.append(dict(gg=gg, j=j, h=h, sl=sl, r0=r0, valid=valid, sc=sc))
        for it in items:
            slope = 2.0 ** (-8.0 * (it["h"] + 1) / N_HEADS)
            sc = jnp.where(it["valid"], it["sc"] * scale - slope * absrel, NEG_INF)
            m = jnp.max(sc, axis=-1, keepdims=True)
            p = jnp.exp(sc - m)
            den = jnp.sum(p, axis=-1, keepdims=True)
            it["p"], it["den"], it["lse"] = p.astype(BF16), den, m + jnp.log(den)
        for it in items:
            pv = _dot(it["p"], window(vp_ref, vc_ref, vn_ref, it["gg"], it["j"], it["sl"]))
            o_scr[it["h"], it["r0"]:it["r0"] + tq, :] = pv / it["den"]
        for gg, j in group:
            lse_tile = jnp.zeros((tq, LANES), F32)
            for it in items:
                if (it["gg"], it["j"]) == (gg, j):
                    lse_tile = jnp.where(lane == it["h"], it["lse"], lse_tile)
            r0 = gg * w + j * tq
            l_scr[r0:r0 + tq, :] = lse_tile
    step = ATTN_PERM // dilation
    nblk = o_ref.shape[0]
    for blk in range(nblk):
        if step == 1:
            rows = slice(blk * half, (blk + 1) * half)
        else:
            rows = pl.ds(blk, half, stride=step)
        for h in range(N_HEADS):
            o_ref[blk, :, h * HEAD_DIM:(h + 1) * HEAD_DIM] = o_scr[h, rows, :].astype(o_ref.dtype)
        lse_ref[blk] = l_scr[rows, :]


def dilated_attention(proj, gi, dilation):
    b, s, _ = proj.shape
    d = D_MODEL
    half = ATTN_HALF
    tp = ATTN_TP
    assert s % tp == 0 and tp % (dilation * half) == 0 and ATTN_PERM % dilation == 0
    nt = s // tp
    w = tp // dilation
    g = 1 if w > half else ATTN_RES_PER_STEP
    nres = dilation // g
    base = gi * 3
    hpt = tp // half

    def cur(which):
        return pl.BlockSpec((None, g * w, d), lambda bi, t, r: (bi, t * nres + r, base + which))

    if g == 1:
        wh = w // half

        def prev(which):
            return pl.BlockSpec((None, half, d),
                                lambda bi, t, r: (bi, jnp.maximum((t - 1) * hpt + (r + 1) * wh - 1, 0), base + which))

        def nxt(which):
            return pl.BlockSpec((None, half, d),
                                lambda bi, t, r: (bi, jnp.minimum(t + 1, nt - 1) * hpt + r * wh, base + which))
    else:
        def prev(which):
            return pl.BlockSpec((None, g * w, d), lambda bi, t, r: (bi, jnp.maximum(t - 1, 0) * nres + r, base + which))

        def nxt(which):
            return pl.BlockSpec((None, g * w, d),
                                lambda bi, t, r: (bi, jnp.minimum(t + 1, nt - 1) * nres + r, base + which))

    perm = ATTN_PERM
    if dilation == perm:
        nblk, lead = g, (b, nt, perm)
        omap = lambda bi, t, r: (bi, t, r, 0, 0)
        oblock = lambda last: (None, None, nblk, half, last)
    else:
        nblk, lead = perm // dilation, (b, nt, perm // dilation, dilation)
        omap = lambda bi, t, r: (bi, t, 0, r, 0, 0)
        oblock = lambda last: (None, None, nblk, None, half, last)

    o, lse = pl.pallas_call(
        functools.partial(_attn_kernel, dilation=dilation, w=w, g=g, sub_len=s // dilation),
        grid=(b, nt, nres),
        in_specs=[cur(0), prev(1), cur(1), nxt(1), prev(2), cur(2), nxt(2)],
        out_specs=[pl.BlockSpec(oblock(d), omap), pl.BlockSpec(oblock(LANES), omap)],
        out_shape=[jax.ShapeDtypeStruct(lead + (half, d), BF16), jax.ShapeDtypeStruct(lead + (half, LANES), F32)],
        scratch_shapes=[pltpu.VMEM((N_HEADS, g * w, HEAD_DIM), F32), pltpu.VMEM((g * w, LANES), F32)],
        compiler_params=_cparams("parallel", "parallel", "parallel"),
        name=f"dilated_attn_g{gi}",
    )(proj, proj, proj, proj, proj, proj, proj)
    return o.reshape(b * s, d), lse.reshape(b * s, LANES)


def _attn_out_kernel(o0_ref, o1_ref, o2_ref, l0_ref, l1_ref, l2_ref, wo_ref, x_ref, out_ref, y_scr):
    parts = []
    for h in range(N_HEADS):
        sl = slice(h * HEAD_DIM, (h + 1) * HEAD_DIM)
        a0, a1, a2 = l0_ref[:, h:h + 1], l1_ref[:, h:h + 1], l2_ref[:, h:h + 1]
        mx = jnp.maximum(jnp.maximum(a0, a1), a2)
        e0, e1, e2 = jnp.exp(a0 - mx), jnp.exp(a1 - mx), jnp.exp(a2 - mx)
        o = (e0 * o0_ref[:, sl].astype(F32) + e1 * o1_ref[:, sl].astype(F32) + e2 * o2_ref[:, sl].astype(F32))
        parts.append((o / (e0 + e1 + e2)).astype(BF16))
    y = _dot(jnp.concatenate(parts, axis=1), wo_ref[...])
    ncol = y_scr.shape[0]
    for c in range(ncol):
        y_scr[c] = y[:, c * LANES:(c + 1) * LANES]
    per = ATTN_TP // ATTN_PERM
    for l in range(per):
        rows = slice(l * ATTN_PERM, (l + 1) * ATTN_PERM)
        for c in range(ncol):
            cols = slice(c * LANES, (c + 1) * LANES)
            out_ref[rows, cols] = x_ref[rows, cols] + y_scr[c, pl.ds(l, ATTN_PERM, stride=per), :]


def attn_out(os, lses, w_out, x):
    n, d = x.shape
    tm = ATTN_TP
    tile = lambda i: (i, 0)
    big = pl.BlockSpec((tm, d), tile)
    small = pl.BlockSpec((tm, LANES), tile)
    return pl.pallas_call(
        _attn_out_kernel,
        grid=(n // tm,),
        in_specs=[big, big, big, small, small, small, pl.BlockSpec((d, d), lambda i: (0, 0)), big],
        out_specs=big,
        out_shape=jax.ShapeDtypeStruct((n, d), F32),
        scratch_shapes=[pltpu.VMEM((d // LANES, tm, LANES), F32)],
        compiler_params=_cparams("parallel"),
        name="attn_out",
    )(*os, *lses, w_out, x)


def _attn_proj_kernel(x_ref, g_ref, w_ref, o_ref, h_ref, acc_ref, *, blocks_per_group):
    j = pl.program_id(1)

    @pl.when(j == 0)
    def _():
        h_ref[...] = _rms(x_ref[...], g_ref[...]).astype(BF16)

    res = _dot(h_ref[...], w_ref[...])
    ncol = acc_ref.shape[0]
    for c in range(ncol):
        acc_ref[c] = res[:, c * LANES:(c + 1) * LANES]
    gi = j // blocks_per_group
    for gidx, (_, dilation) in enumerate(ATTN_GROUPS):
        @pl.when(gi == gidx)
        def _():
            if dilation == 1:
                o_ref[...] = res.astype(o_ref.dtype)
            else:
                w = ATTN_TP // dilation
                for r in range(dilation):
                    for c in range(ncol):
                        o_ref[r * w:(r + 1) * w, c * LANES:(c + 1) * LANES] = (
                            acc_ref[c, pl.ds(r, w, stride=dilation), :].astype(o_ref.dtype))


def attn_proj(x, g, w, tn=512):
    n, d = x.shape
    nout = w.shape[1]
    tm = ATTN_TP
    group_cols = nout // len(ATTN_GROUPS)
    assert group_cols % tn == 0
    return pl.pallas_call(
        functools.partial(_attn_proj_kernel, blocks_per_group=group_cols // tn),
        grid=(n // tm, nout // tn),
        in_specs=[pl.BlockSpec((tm, d), lambda i, j: (i, 0)),
                  pl.BlockSpec((1, d), lambda i, j: (0, 0)),
                  pl.BlockSpec((d, tn), lambda i, j: (0, j))],
        out_specs=pl.BlockSpec((tm, tn), lambda i, j: (i, j)),
        out_shape=jax.ShapeDtypeStruct((n, nout), BF16),
        scratch_shapes=[pltpu.VMEM((tm, d), BF16), pltpu.VMEM((tn // LANES, tm, LANES), F32)],
        compiler_params=_cparams("parallel", "arbitrary"),
        name="attn_proj",
    )(x, g.reshape(1, d), w)


def _route_kernel(x_ref, g_ref, r_ref, h_ref, aff_ref):
    h = _rms(x_ref[...], g_ref[...])
    h_ref[...] = h.astype(BF16)
    logits = jnp.dot(h, r_ref[...], precision=lax.Precision.HIGHEST, preferred_element_type=F32)
    lt = logits.T[:N_EXPERTS, :]
    m = jnp.max(lt, axis=0, keepdims=True)
    e = jnp.exp(lt - m)
    aff_ref[...] = e / jnp.sum(e, axis=0, keepdims=True)


def moe_route(x, g, router, tm=512):
    n, d = x.shape
    rpad = jnp.zeros((d, LANES), F32).at[:, :N_EXPERTS].set(router)
    return pl.pallas_call(
        _route_kernel,
        grid=(n // tm,),
        in_specs=[pl.BlockSpec((tm, d), lambda i: (i, 0)),
                  pl.BlockSpec((1, d), lambda i: (0, 0)),
                  pl.BlockSpec((d, LANES), lambda i: (0, 0))],
        out_specs=[pl.BlockSpec((tm, d), lambda i: (i, 0)),
                   pl.BlockSpec((N_EXPERTS, tm), lambda i: (0, i))],
        out_shape=[jax.ShapeDtypeStruct((n, d), BF16), jax.ShapeDtypeStruct((N_EXPERTS, n), F32)],
        compiler_params=_cparams("parallel"),
        name="moe_route",
    )(x, g.reshape(1, d), rpad)


def _select_kernel(aff_ref, sel_ref, pos_ref, *, cap):
    e, n = aff_ref.shape
    bits = pltpu.bitcast(aff_ref[...], I32)
    tidx = lax.broadcasted_iota(I32, (e, n), 1)

    def count(mask_i32):
        return jnp.sum(mask_i32.astype(F32), axis=1, keepdims=True).astype(I32)

    def thr_body(it, thr):
        cand = thr | jnp.left_shift(jnp.int32(1), 30 - it)
        cnt = count(jnp.where(bits >= cand, 1, 0))
        return jnp.where(cnt >= cap, cand, thr)

    thr = lax.fori_loop(0, 31, thr_body, jnp.zeros((e, 1), I32))
    gt = jnp.where(bits > thr, 1, 0)
    eq = jnp.where(bits == thr, 1, 0)
    need = cap - count(gt)
    nbits = max(1, (n - 1).bit_length())

    def tie_body(it, j):
        cand = j | jnp.left_shift(jnp.int32(1), nbits - 1 - it)
        cnt = count(jnp.where(tidx < cand, eq, 0))
        return jnp.where(cnt < need, cand, j)

    jlast = lax.fori_loop(0, nbits, tie_body, jnp.zeros((e, 1), I32))
    sel = gt + jnp.where(tidx <= jlast, eq, 0)
    sel_ref[...] = sel
    rr = lax.broadcasted_iota(I32, (LANES, LANES), 0)
    cc = lax.broadcasted_iota(I32, (LANES, LANES), 1)
    upper = jnp.where(rr <= cc, 1.0, 0.0).astype(BF16)

    def blk_body(bi, run):
        off = pl.multiple_of(bi * LANES, LANES)
        sb = sel_ref[:, pl.ds(off, LANES)]
        inc = _dot(sb.astype(F32).astype(BF16), upper).astype(I32)
        pos_ref[:, pl.ds(off, LANES)] = inc - sb + run
        return run + inc[:, LANES - 1:LANES]

    lax.fori_loop(0, n // LANES, blk_body, jnp.zeros((e, 1), I32))


def moe_select(aff_t, cap):
    e, n = aff_t.shape
    return pl.pallas_call(
        functools.partial(_select_kernel, cap=cap),
        grid=(1,),
        in_specs=[pl.BlockSpec((e, n), lambda i: (0, 0))],
        out_specs=[pl.BlockSpec((e, n), lambda i: (0, 0)), pl.BlockSpec((e, n), lambda i: (0, 0))],
        out_shape=[jax.ShapeDtypeStruct((e, n), I32), jax.ShapeDtypeStruct((e, n), I32)],
        compiler_params=_cparams("arbitrary"),
        name="moe_select",
    )(aff_t)


def _ffn_kernel(x_ref, gate_ref, wg_ref, wu_ref, wd_ref, o_ref):
    x = x_ref[...]
    acc = None
    for f in range(EXPERT_FF // FF_CHUNK):
        fs = slice(f * FF_CHUNK, (f + 1) * FF_CHUNK)
        hid = (_silu(_dot(x, wg_ref[:, fs])) * _dot(x, wu_ref[:, fs])).astype(BF16)
        part = _dot(hid, wd_ref[fs, :])
        acc = part if acc is None else acc + part
    o_ref[...] = (acc * gate_ref[...]).astype(o_ref.dtype)


def moe_ffn(xe, gate, w_gate, w_up, w_down, tm=512):
    e, c, d = xe.shape
    f = w_gate.shape[2]
    return pl.pallas_call(
        _ffn_kernel,
        grid=(e, c // tm),
        in_specs=[pl.BlockSpec((None, tm, d), lambda ei, j: (ei, j, 0)),
                  pl.BlockSpec((None, tm, 1), lambda ei, j: (ei, j, 0)),
                  pl.BlockSpec((None, d, f), lambda ei, j: (ei, 0, 0)),
                  pl.BlockSpec((None, d, f), lambda ei, j: (ei, 0, 0)),
                  pl.BlockSpec((None, f, d), lambda ei, j: (ei, 0, 0))],
        out_specs=pl.BlockSpec((None, tm, d), lambda ei, j: (ei, j, 0)),
        out_shape=jax.ShapeDtypeStruct((e, c, d), BF16),
        compiler_params=_cparams("parallel", "arbitrary"),
        name="moe_ffn",
    )(xe, gate, w_gate, w_up, w_down)


def _combine_dmas(tab_ref, ye_ref, zbuf_ref, sem, i, cap, total, tile):
    copies, bases = [], []
    off = jnp.int32(0)
    max_shift = (tile // BF16_SUBLANES + 1).bit_length() - 1
    for e in range(N_EXPERTS):
        p0 = tab_ref[i * N_EXPERTS + e]
        cnt = tab_ref[(i + 1) * N_EXPERTS + e] - p0
        g0 = e * cap + p0
        a = (g0 // BF16_SUBLANES) * BF16_SUBLANES
        nrow = jnp.where(cnt > 0, ((g0 - a + cnt + BF16_SUBLANES - 1) // BF16_SUBLANES) * BF16_SUBLANES, 0)
        a = jnp.minimum(a, total - nrow)
        bases.append(off - a + e * cap)
        src, dst = a, off
        for sh in range(max_shift, -1, -1):
            size = BF16_SUBLANES << sh
            take = (nrow & size) != 0
            cp = pltpu.make_async_copy(ye_ref.at[pl.ds(pl.multiple_of(src, BF16_SUBLANES), size)],
                                       zbuf_ref.at[pl.ds(pl.multiple_of(dst, BF16_SUBLANES), size)], sem)
            copies.append((take, cp))
            step = jnp.where(take, size, 0)
            src, dst = src + step, dst + step
        off = off + nrow
    return copies, bases, off


def _combine_kernel(tab_ref, x_ref, sel_ref, pos_ref, ye_ref, g_ref, o_ref, zbuf_ref, sem,
                    *, cap, total, tile, final_norm):
    i = pl.program_id(0)
    slot = i % 2

    def start_tile(ti, sl):
        for take, cp in _combine_dmas(tab_ref, ye_ref, zbuf_ref.at[sl], sem.at[sl], ti, cap, total, tile)[0]:
            @pl.when(take)
            def _():
                cp.start()

    @pl.when(i == 0)
    def _():
        zbuf_ref[...] = jnp.zeros_like(zbuf_ref)
        start_tile(0, 0)

    @pl.when(i + 1 < pl.num_programs(0))
    def _():
        start_tile(i + 1, 1 - slot)

    zcur_ref = zbuf_ref.at[slot]
    copies, bases, k_tot = _combine_dmas(tab_ref, ye_ref, zcur_ref, sem.at[slot], i, cap, total, tile)
    lane_e = lax.broadcasted_iota(I32, (1, N_EXPERTS), 1)
    cvec = jnp.zeros((1, N_EXPERTS), I32)
    for e in range(N_EXPERTS):
        cvec = jnp.where(lane_e == e, bases[e], cvec)
    col = jnp.where(sel_ref[...] > 0, pos_ref[...] + cvec, -1)
    o_ref[...] = x_ref[...]

    for take, cp in copies:
        @pl.when(take)
        def _():
            cp.wait()

    kc = COMBINE_KCHUNK

    def chunk_body(ci, carry):
        base = pl.multiple_of(ci * kc, kc)
        z = zcur_ref[pl.ds(base, kc), :]
        lanes = lax.broadcasted_iota(I32, (tile, kc), 1) + base
        p = jnp.zeros((tile, kc), F32)
        for e in range(N_EXPERTS):
            p = jnp.where(col[:, e:e + 1] == lanes, 1.0, p)
        o_ref[...] += _dot(p.astype(BF16), z)
        return carry

    lax.fori_loop(0, (k_tot + kc - 1) // kc, chunk_body, 0)
    if final_norm:
        o_ref[...] = _rms(o_ref[...], g_ref[...])


def moe_combine(x, sel_t, pos_t, tab, ye, cap, norm_g=None, tile=COMBINE_TILE):
    n, d = x.shape
    total = ye.shape[0]
    kmax = N_EXPERTS * (tile + BF16_SUBLANES)
    kmax = -(-kmax // COMBINE_KCHUNK) * COMBINE_KCHUNK
    final_norm = norm_g is not None
    g = (norm_g if final_norm else jnp.ones((d,), F32)).reshape(1, d)
    grid_spec = pltpu.PrefetchScalarGridSpec(
        num_scalar_prefetch=1,
        grid=(n // tile,),
        in_specs=[pl.BlockSpec((tile, d), lambda i, tab: (i, 0)),
                  pl.BlockSpec((tile, N_EXPERTS), lambda i, tab: (i, 0)),
                  pl.BlockSpec((tile, N_EXPERTS), lambda i, tab: (i, 0)),
                  pl.BlockSpec(memory_space=pl.ANY),
                  pl.BlockSpec((1, d), lambda i, tab: (0, 0))],
        out_specs=pl.BlockSpec((tile, d), lambda i, tab: (i, 0)),
        scratch_shapes=[pltpu.VMEM((2, kmax, d), BF16), pltpu.SemaphoreType.DMA((2,))],
    )
    return pl.pallas_call(
        functools.partial(_combine_kernel, cap=cap, total=total, tile=tile, final_norm=final_norm),
        grid_spec=grid_spec,
        out_shape=jax.ShapeDtypeStruct((n, d), F32),
        compiler_params=_cparams("arbitrary"),
        name="moe_combine",
    )(tab, x, sel_t, pos_t, ye, g)


def expert_choice_moe(x, norm_g, router, w_gate, w_up, w_down, final_g=None):
    n, d = x.shape
    cap = max(1, CAPACITY_FACTOR * n // N_EXPERTS)
    hb, aff_t = moe_route(x, norm_g, router)
    sel, pos = moe_select(aff_t, cap)
    tok = jnp.arange(n, dtype=I32)[None, :]
    idx = jnp.sort(jnp.where(sel > 0, tok, tok + n), axis=1)[:, :cap]
    gate = jnp.take_along_axis(aff_t, idx, axis=1)[..., None]
    xe = jnp.take(hb, idx.reshape(-1), axis=0).reshape(N_EXPERTS, cap, d)
    ye = moe_ffn(xe, gate, w_gate, w_up, w_down).reshape(N_EXPERTS * cap, d)
    tab = jnp.concatenate([pos[:, ::COMBINE_TILE].T, jnp.full((1, N_EXPERTS), cap, I32)], axis=0).reshape(-1)
    return moe_combine(x, sel.T, pos.T, tab, ye, cap, norm_g=final_g)


def _trunk(x, p):
    b, s, d = x.shape
    n = b * s
    x = x.reshape(n, d)
    for i in range(DEPTH):
        kind, j = i % N_MIXERS, i // N_MIXERS
        if kind == 0:
            u = norm_glu(x, p["norm_mix"][i], p["conv_w_glu"][j], p["conv_b_glu"][j])
            x = conv_out(u.reshape(b, s, d), x.reshape(b, s, d), p["conv_w_dw"][j], p["conv_b_dw"][j],
                         p["conv_ln_g"][j], p["conv_ln_b"][j], p["conv_w_out"][j], p["conv_b_out"][j]).reshape(n, d)
        elif kind == 1:
            proj = norm_linear(x, p["norm_mix"][i], p["dn_w_in"][j], F32)
            q, k, v, bg = dn_prep(proj.reshape(b, s, -1), p["dn_w_conv"][j], p["dn_a_log"][j], p["dn_dt_bias"][j])
            o_f, o_b = delta_rule(q, k, v, bg)
            x = dn_out(o_f.reshape(n, d), o_b.reshape(n, d), proj, p["dn_norm_g"][j], p["dn_w_out"][j], x)
        else:
            proj = attn_proj(x, p["norm_mix"][i], p["attn_w_in"][j]).reshape(b, s, -1)
            os, lses = [], []
            for gi, (window, dilation) in enumerate(ATTN_GROUPS):
                assert window // (2 * dilation) == ATTN_HALF
                o, lse = dilated_attention(proj, gi, dilation)
                os.append(o)
                lses.append(lse)
            x = attn_out(os, lses, p["attn_w_out"][j], x)
        x = expert_choice_moe(x, p["norm_ffn"][i], p["moe_router"][i], p["moe_w_gate"][i], p["moe_w_up"][i],
                              p["moe_w_down"][i], final_g=p["norm_final"] if i == DEPTH - 1 else None)
    return x.reshape(b, s, d)


def kernel(x_prompt, x_sample, norm_mix, norm_ffn, norm_final, conv_w_glu, conv_b_glu, conv_w_dw, conv_b_dw,
           conv_ln_g, conv_ln_b, conv_w_out, conv_b_out, dn_w_in, dn_w_conv, dn_a_log, dn_dt_bias, dn_norm_g,
           dn_w_out, attn_w_in, attn_w_out, moe_router, moe_w_gate, moe_w_up, moe_w_down):
    dn_pad = (-dn_w_in.shape[-1]) % 512
    p = dict(
        norm_mix=norm_mix, norm_ffn=norm_ffn, norm_final=norm_final,
        conv_w_glu=conv_w_glu.astype(BF16), conv_b_glu=conv_b_glu, conv_w_dw=conv_w_dw, conv_b_dw=conv_b_dw,
        conv_ln_g=conv_ln_g, conv_ln_b=conv_ln_b, conv_w_out=conv_w_out.astype(BF16), conv_b_out=conv_b_out,
        dn_w_in=jnp.pad(dn_w_in, ((0, 0), (0, 0), (0, dn_pad))).astype(BF16), dn_w_conv=dn_w_conv,
        dn_a_log=dn_a_log, dn_dt_bias=dn_dt_bias, dn_norm_g=dn_norm_g, dn_w_out=dn_w_out.astype(BF16),
        attn_w_in=attn_w_in.astype(BF16), attn_w_out=attn_w_out.astype(BF16),
        moe_router=moe_router, moe_w_gate=moe_w_gate.astype(BF16), moe_w_up=moe_w_up.astype(BF16),
        moe_w_down=moe_w_down.astype(BF16),
    )
    return (_trunk(x_prompt, p), _trunk(x_sample, p))
```

```python
import functools
import math

import jax
import jax.numpy as jnp
from jax import lax
from jax.experimental import pallas as pl
from jax.experimental.pallas import tpu as pltpu

F32 = jnp.float32
BF16 = jnp.bfloat16
I32 = jnp.int32

D_MODEL = 1024
DEPTH = 4
N_MIXERS = 3
RMS_EPS = 1e-6
LN_EPS = 1e-5
NEG_INF = -1e30
CONV_WIDTH = 31
HEAD_DIM = 128
N_HEADS = D_MODEL // HEAD_DIM
SHORT_CONV = 5
ATTN_GROUPS = ((128, 1), (512, 4), (2048, 16))
N_EXPERTS = 16
EXPERT_FF = 2 * D_MODEL
CAPACITY_FACTOR = 2

VMEM_LIMIT_BYTES = 56 * 1024 * 1024
LANES = 128
BF16_SUBLANES = 16

DN_CHUNK = 64
DN_SPLIT_PASSES = False
ATTN_HALF = 64
ATTN_TQ = 128
ATTN_PERM = 16
ATTN_TP = ATTN_PERM * ATTN_HALF
ATTN_RES_PER_STEP = 4
FF_CHUNK = 512
COMBINE_TILE = 256
COMBINE_KCHUNK = 256


def _cparams(*sem):
    return pltpu.CompilerParams(dimension_semantics=sem, vmem_limit_bytes=VMEM_LIMIT_BYTES)


def _rms(x, g):
    return x * lax.rsqrt(jnp.mean(x * x, axis=-1, keepdims=True) + RMS_EPS) * g


def _silu(x):
    return x * jax.nn.sigmoid(x)


def _dot(a, b):
    return jnp.dot(a, b, preferred_element_type=F32)


def _dot_nt(a, b):
    return lax.dot_general(a, b, (((1,), (1,)), ((), ())), preferred_element_type=F32)


def _dot_tn(a, b):
    return lax.dot_general(a, b, (((0,), (0,)), ((), ())), preferred_element_type=F32)


def _norm_linear_kernel(x_ref, g_ref, w_ref, o_ref, h_ref):
    @pl.when(pl.program_id(1) == 0)
    def _():
        h_ref[...] = _rms(x_ref[...], g_ref[...]).astype(BF16)

    o_ref[...] = _dot(h_ref[...], w_ref[...]).astype(o_ref.dtype)


def norm_linear(x, g, w, out_dtype, tm=1024, tn=512):
    n, d = x.shape
    nout = w.shape[1]
    return pl.pallas_call(
        _norm_linear_kernel,
        grid=(n // tm, nout // tn),
        in_specs=[pl.BlockSpec((tm, d), lambda i, j: (i, 0)),
                  pl.BlockSpec((1, d), lambda i, j: (0, 0)),
                  pl.BlockSpec((d, tn), lambda i, j: (0, j))],
        out_specs=pl.BlockSpec((tm, tn), lambda i, j: (i, j)),
        out_shape=jax.ShapeDtypeStruct((n, nout), out_dtype),
        scratch_shapes=[pltpu.VMEM((tm, d), BF16)],
        compiler_params=_cparams("parallel", "arbitrary"),
        name="norm_linear",
    )(x, g.reshape(1, d), w)


def _norm_glu_kernel(x_ref, g_ref, wa_ref, wg_ref, ba_ref, bg_ref, o_ref, h_ref):
    @pl.when(pl.program_id(1) == 0)
    def _():
        h_ref[...] = _rms(x_ref[...], g_ref[...]).astype(BF16)

    h = h_ref[...]
    a = _dot(h, wa_ref[...]) + ba_ref[...]
    gate = _dot(h, wg_ref[...]) + bg_ref[...]
    o_ref[...] = a * jax.nn.sigmoid(gate)


def norm_glu(x, g, w, b, tm=1024, tn=512):
    n, d = x.shape
    nb = d // tn
    b2 = b.reshape(1, 2 * d)
    return pl.pallas_call(
        _norm_glu_kernel,
        grid=(n // tm, nb),
        in_specs=[pl.BlockSpec((tm, d), lambda i, j: (i, 0)),
                  pl.BlockSpec((1, d), lambda i, j: (0, 0)),
                  pl.BlockSpec((d, tn), lambda i, j: (0, j)),
                  pl.BlockSpec((d, tn), lambda i, j: (0, j + nb)),
                  pl.BlockSpec((1, tn), lambda i, j: (0, j)),
                  pl.BlockSpec((1, tn), lambda i, j: (0, j + nb))],
        out_specs=pl.BlockSpec((tm, tn), lambda i, j: (i, j)),
        out_shape=jax.ShapeDtypeStruct((n, d), F32),
        scratch_shapes=[pltpu.VMEM((tm, d), BF16)],
        compiler_params=_cparams("parallel", "arbitrary"),
        name="norm_glu",
    )(x, g.reshape(1, d), w, w, b2, b2)


CONV_HALO = 16


def _conv_out_kernel(up_ref, uc_ref, un_ref, wdw_ref, bdw_ref, lng_ref, lnb_ref, wo_ref, bo_ref, x_ref,
                     o_ref, win_ref, shift_ref, *, ts):
    i = pl.program_id(1)
    last = pl.num_programs(1) - 1
    win_ref[0:CONV_HALO, :] = jnp.where(i > 0, up_ref[...], 0.0)
    win_ref[CONV_HALO:CONV_HALO + ts, :] = uc_ref[...]
    win_ref[CONV_HALO + ts:, :] = jnp.where(i < last, un_ref[...], 0.0)
    for s in range(1, 8):
        shift_ref[s - 1] = win_ref[pl.ds(s, ts + 24), :]
    acc = jnp.zeros((ts, D_MODEL), F32) + bdw_ref[...]
    for s in range(8):
        for a in range(4):
            j = 8 * a + s - 1
            if 0 <= j < CONV_WIDTH:
                rows = slice(8 * a, 8 * a + ts)
                tap = win_ref[rows, :] if s == 0 else shift_ref[s - 1, rows, :]
                acc = acc + wdw_ref[j:j + 1, :] * tap
    mu = jnp.mean(acc, axis=-1, keepdims=True)
    cen = acc - mu
    var = jnp.mean(cen * cen, axis=-1, keepdims=True)
    u = cen * lax.rsqrt(var + LN_EPS) * lng_ref[...] + lnb_ref[...]
    y = _silu(u).astype(BF16)
    o_ref[...] = x_ref[...] + _dot(y, wo_ref[...]) + bo_ref[...]


def conv_out(u, x, w_dw, b_dw, ln_g, ln_b, w_out, b_out, ts=256):
    b, s, d = u.shape
    hb = ts // CONV_HALO
    nblk = s // CONV_HALO
    wdw = jnp.concatenate([w_dw, jnp.zeros((1, d), F32)], axis=0)
    row = lambda v: v.reshape(1, d)
    return pl.pallas_call(
        functools.partial(_conv_out_kernel, ts=ts),
        grid=(b, s // ts),
        in_specs=[pl.BlockSpec((None, CONV_HALO, d), lambda bi, i: (bi, jnp.maximum(i * hb - 1, 0), 0)),
                  pl.BlockSpec((None, ts, d), lambda bi, i: (bi, i, 0)),
                  pl.BlockSpec((None, CONV_HALO, d), lambda bi, i: (bi, jnp.minimum((i + 1) * hb, nblk - 1), 0)),
                  pl.BlockSpec((CONV_WIDTH + 1, d), lambda bi, i: (0, 0)),
                  pl.BlockSpec((1, d), lambda bi, i: (0, 0)),
                  pl.BlockSpec((1, d), lambda bi, i: (0, 0)),
                  pl.BlockSpec((1, d), lambda bi, i: (0, 0)),
                  pl.BlockSpec((d, d), lambda bi, i: (0, 0)),
                  pl.BlockSpec((1, d), lambda bi, i: (0, 0)),
                  pl.BlockSpec((None, ts, d), lambda bi, i: (bi, i, 0))],
        out_specs=pl.BlockSpec((None, ts, d), lambda bi, i: (bi, i, 0)),
        out_shape=jax.ShapeDtypeStruct((b, s, d), F32),
        scratch_shapes=[pltpu.VMEM((ts + 2 * CONV_HALO, d), F32), pltpu.VMEM((7, ts + 24, d), F32)],
        compiler_params=_cparams("parallel", "parallel"),
        name="conv_out",
    )(u, u, u, wdw, row(b_dw), row(ln_g), row(ln_b), w_out, row(b_out), x)


DN_HALO = 8
DN_BG = 32


def _dn_prep_kernel(pp_ref, pc_ref, pn_ref, ba_ref, wc_ref, alog_ref, dtb_ref,
                    q_ref, k_ref, v_ref, bg_ref, win_ref, *, ts):
    i = pl.program_id(1)
    last = pl.num_programs(1) - 1
    w3 = 3 * D_MODEL
    win_ref[0:DN_HALO, :] = jnp.where(i > 0, pp_ref[...], 0.0)
    win_ref[DN_HALO:DN_HALO + ts, :] = pc_ref[...]
    win_ref[DN_HALO + ts:, :] = jnp.where(i < last, pn_ref[...], 0.0)
    left = (SHORT_CONV - 1) // 2
    acc = jnp.zeros((ts, w3), F32)
    for j in range(SHORT_CONV):
        acc = acc + wc_ref[j:j + 1, :] * win_ref[pl.ds(DN_HALO - left + j, ts), :]
    qkv = _silu(acc)

    def l2n(t):
        return t * lax.rsqrt(jnp.sum(t * t, axis=-1, keepdims=True) + RMS_EPS)

    for h in range(N_HEADS):
        sl = slice(h * HEAD_DIM, (h + 1) * HEAD_DIM)
        q_ref[:, sl] = l2n(qkv[:, sl]) * (HEAD_DIM ** -0.5)
        k_ref[:, sl] = l2n(qkv[:, D_MODEL + h * HEAD_DIM:D_MODEL + (h + 1) * HEAD_DIM])
    v_ref[...] = qkv[:, 2 * D_MODEL:]
    ba = ba_ref[:, :DN_BG]
    lane = lax.broadcasted_iota(I32, (ts, DN_BG), 1)
    beta = jax.nn.sigmoid(ba)
    g = -jnp.exp(alog_ref[...]) * jax.nn.softplus(ba + dtb_ref[...])
    bg_ref[...] = jnp.where(lane < DN_BG // 2, beta, g)


def dn_prep(proj, w_conv, a_log, dt_bias, ts=256):
    b, s, _ = proj.shape
    d = D_MODEL
    hb = ts // DN_HALO
    nblk = s // DN_HALO
    zeros16 = jnp.zeros((16,), F32)
    alog = jnp.concatenate([zeros16, a_log.reshape(-1)]).reshape(1, DN_BG)
    dtb = jnp.concatenate([zeros16, dt_bias.reshape(-1)]).reshape(1, DN_BG)
    bacol = 4 * d // LANES
    tile = lambda bi, i: (bi, i, 0)
    return pl.pallas_call(
        functools.partial(_dn_prep_kernel, ts=ts),
        grid=(b, s // ts),
        in_specs=[pl.BlockSpec((None, DN_HALO, 3 * d), lambda bi, i: (bi, jnp.maximum(i * hb - 1, 0), 0)),
                  pl.BlockSpec((None, ts, 3 * d), tile),
                  pl.BlockSpec((None, DN_HALO, 3 * d), lambda bi, i: (bi, jnp.minimum((i + 1) * hb, nblk - 1), 0)),
                  pl.BlockSpec((None, ts, LANES), lambda bi, i: (bi, i, bacol)),
                  pl.BlockSpec((SHORT_CONV, 3 * d), lambda bi, i: (0, 0)),
                  pl.BlockSpec((1, DN_BG), lambda bi, i: (0, 0)),
                  pl.BlockSpec((1, DN_BG), lambda bi, i: (0, 0))],
        out_specs=[pl.BlockSpec((None, ts, d), tile),
                   pl.BlockSpec((None, ts, d), tile),
                   pl.BlockSpec((None, ts, d), tile),
                   pl.BlockSpec((None, ts, DN_BG), tile)],
        out_shape=[jax.ShapeDtypeStruct((b, s, d), F32)] * 3 + [jax.ShapeDtypeStruct((b, s, DN_BG), F32)],
        scratch_shapes=[pltpu.VMEM((ts + 2 * DN_HALO, 3 * d), F32)],
        compiler_params=_cparams("parallel", "parallel"),
        name="dn_prep",
    )(proj, proj, proj, proj, w_conv, alog, dtb)


def _delta_kernel(qf_ref, kf_ref, vf_ref, bgf_ref, qb_ref, kb_ref, vb_ref, bgb_ref,
                  of_ref, ob_ref, state_ref, *, chunk):
    @pl.when(pl.program_id(1) == 0)
    def _():
        state_ref[...] = jnp.zeros_like(state_ref)

    c = chunk
    dh = HEAD_DIM
    ns = 2 * dh
    r = lax.broadcasted_iota(I32, (c, c), 0)
    s = lax.broadcasted_iota(I32, (c, c), 1)
    eye = jnp.where(lax.broadcasted_iota(I32, (DN_BG, DN_BG), 0) == lax.broadcasted_iota(I32, (DN_BG, DN_BG), 1),
                    1.0, 0.0)
    hi = lax.Precision.HIGHEST
    chains = []
    for di, (q_ref, k_ref, v_ref, bg_ref, o_ref) in enumerate(
            ((qf_ref, kf_ref, vf_ref, bgf_ref, of_ref), (qb_ref, kb_ref, vb_ref, bgb_ref, ob_ref))):
        reverse = di == 1
        incl = (r <= s) if reverse else (r >= s)
        strict = (r < s) if reverse else (r > s)
        bg = bg_ref[...]
        gc_cols = jnp.dot(jnp.where(incl, 1.0, 0.0), bg, precision=hi, preferred_element_type=F32)
        gc_rows = lax.dot_general(eye, gc_cols, (((1,), (1,)), ((), ())),
                                  precision=hi, preferred_element_type=F32)
        for h in range(N_HEADS):
            jb = di * N_HEADS + h
            jg = DN_BG // 2 + jb
            gc_col = gc_cols[:, jg:jg + 1]
            chains.append(dict(
                sl=slice(h * dh, (h + 1) * dh), jb=jb, q_ref=q_ref, k_ref=k_ref, v_ref=v_ref, o_ref=o_ref,
                incl=incl, strict=strict, beta=bg[:, jb:jb + 1], gc_col=gc_col, gc_row=gc_rows[jg:jg + 1, :],
                g_tot=gc_col[0:1, :] if reverse else gc_col[c - 1:c, :]))

    for ch in chains:
        q, k, v = ch["q_ref"][:, ch["sl"]], ch["k_ref"][:, ch["sl"]], ch["v_ref"][:, ch["sl"]]
        decay = jnp.exp(jnp.where(ch["incl"], ch["gc_col"] - ch["gc_row"], NEG_INF))
        kb = k * ch["beta"]
        egc = jnp.exp(ch["gc_col"])
        qk_kk = _dot_nt(jnp.concatenate([q, kb], axis=0).astype(BF16), k.astype(BF16))
        ch["a"] = jnp.where(ch["incl"], qk_kk[:c] * decay, 0.0).astype(BF16)
        m = jnp.where(ch["strict"], -qk_kk[c:] * decay, 0.0)
        ch["x"] = jnp.concatenate([v * ch["beta"], kb * egc, m], axis=1)
        ch["qd"] = (q * egc).astype(BF16)
        ch["kd"] = (k * jnp.exp(ch["g_tot"] - ch["gc_col"])).astype(BF16)

    n_levels = max(1, (c - 1).bit_length())
    for lvl in range(n_levels):
        width = ns + c if lvl + 1 < n_levels else ns
        for ch in chains:
            x = ch["x"]
            xh = x.astype(BF16)
            if DN_SPLIT_PASSES:
                xl = (x - xh.astype(F32)).astype(BF16)
                lhs = jnp.concatenate([xh[:, ns:], xh[:, ns:], xl[:, ns:]], axis=1)
                rhs = jnp.concatenate([xh[:, :width], xl[:, :width], xh[:, :width]], axis=0)
            else:
                lhs, rhs = xh[:, ns:], xh[:, :width]
            prod = _dot(lhs, rhs)
            sol = x[:, :ns] + prod[:, :ns]
            ch["x"] = jnp.concatenate([sol, prod[:, ns:]], axis=1) if width > ns else sol

    for ch in chains:
        ch["state"] = state_ref[ch["jb"]]
        lhs = jnp.concatenate([ch["x"][:, dh:].astype(BF16), ch["qd"]], axis=0)
        ch["ws"] = _dot(lhs, ch["state"].astype(BF16))
    for ch in chains:
        ch["vb"] = (ch["x"][:, :dh] - ch["ws"][:c]).astype(BF16)
        ch["o_ref"][:, ch["sl"]] = ch["ws"][c:] + _dot(ch["a"], ch["vb"])
    for ch in chains:
        state_ref[ch["jb"]] = ch["state"] * jnp.exp(ch["g_tot"]) + _dot_tn(ch["kd"], ch["vb"])


def delta_rule(q, k, v, bg, chunk=DN_CHUNK):
    b, s, d = q.shape
    nc = s // chunk
    fwd = lambda bi, c: (bi, c, 0)
    bwd = lambda bi, c: (bi, nc - 1 - c, 0)
    big = lambda im: pl.BlockSpec((None, chunk, d), im)
    return pl.pallas_call(
        functools.partial(_delta_kernel, chunk=chunk),
        grid=(b, nc),
        in_specs=[big(fwd), big(fwd), big(fwd), pl.BlockSpec((None, chunk, DN_BG), fwd),
                  big(bwd), big(bwd), big(bwd), pl.BlockSpec((None, chunk, DN_BG), bwd)],
        out_specs=[big(fwd), big(bwd)],
        out_shape=[jax.ShapeDtypeStruct((b, s, d), F32)] * 2,
        scratch_shapes=[pltpu.VMEM((2 * N_HEADS, HEAD_DIM, HEAD_DIM), F32)],
        compiler_params=_cparams("parallel", "arbitrary"),
        name="delta_rule",
    )(q, k, v, bg, q, k, v, bg)


def _dn_out_kernel(of_ref, ob_ref, z_ref, ng_ref, wo_ref, x_ref, o_ref):
    parts = []
    for h in range(N_HEADS):
        sl = slice(h * HEAD_DIM, (h + 1) * HEAD_DIM)
        o = of_ref[:, sl] + ob_ref[:, sl]
        o = o * lax.rsqrt(jnp.mean(o * o, axis=-1, keepdims=True) + RMS_EPS) * ng_ref[...] * _silu(z_ref[:, sl])
        parts.append(o.astype(BF16))
    o_ref[...] = x_ref[...] + _dot(jnp.concatenate(parts, axis=1), wo_ref[...])


def dn_out(o_f, o_b, proj, norm_g, w_out, x, tm=512):
    n, d = x.shape
    tile = lambda i: (i, 0)
    return pl.pallas_call(
        _dn_out_kernel,
        grid=(n // tm,),
        in_specs=[pl.BlockSpec((tm, d), tile), pl.BlockSpec((tm, d), tile),
                  pl.BlockSpec((tm, d), lambda i: (i, 3)),
                  pl.BlockSpec((1, HEAD_DIM), lambda i: (0, 0)),
                  pl.BlockSpec((d, d), lambda i: (0, 0)),
                  pl.BlockSpec((tm, d), tile)],
        out_specs=pl.BlockSpec((tm, d), tile),
        out_shape=jax.ShapeDtypeStruct((n, d), F32),
        compiler_params=_cparams("parallel"),
        name="dn_out",
    )(o_f, o_b, proj, norm_g.reshape(1, HEAD_DIM), w_out, x)


def _attn_kernel(q_ref, kp_ref, kc_ref, kn_ref, vp_ref, vc_ref, vn_ref, o_ref, lse_ref, o_scr, l_scr,
                 *, dilation, w, g, sub_len):
    t = pl.program_id(1)
    half = ATTN_HALF
    tq = min(ATTN_TQ, w)
    nsub = w // tq
    tk = tq + 2 * half
    row = lax.broadcasted_iota(I32, (tq, tk), 0)
    col = lax.broadcasted_iota(I32, (tq, tk), 1)
    rel = col - half - row
    band = jnp.abs(rel) <= half
    absrel = jnp.abs(rel).astype(F32) * float(dilation)
    lane = lax.broadcasted_iota(I32, (tq, LANES), 1)
    scale = HEAD_DIM ** -0.5

    def window(p_ref, c_ref, n_ref, gg, j, sl):
        r0 = gg * w + j * tq
        hrow = slice(gg * half, (gg + 1) * half) if g > 1 else slice(None)
        before = p_ref[hrow, sl] if j == 0 else c_ref[r0 - half:r0, sl]
        after = n_ref[hrow, sl] if j == nsub - 1 else c_ref[r0 + tq:r0 + tq + half, sl]
        return jnp.concatenate([before, c_ref[r0:r0 + tq, sl], after], axis=0)

    blocks = [(gg, j) for gg in range(g) for j in range(nsub)]
    groups = [blocks] if g > 1 else [[blk] for blk in blocks]
    for group in groups:
        items = []
        for gg, j in group:
            r0 = gg * w + j * tq
            key = t * w + j * tq - half + col
            valid = band & (key >= 0) & (key < sub_len)
            for h in range(N_HEADS):
                sl = slice(h * HEAD_DIM, (h + 1) * HEAD_DIM)
                sc = _dot_nt(q_ref[r0:r0 + tq, sl], window(kp_ref, kc_ref, kn_ref, gg, j, sl))
                items.append(dict(gg=gg, j=j, h=h, sl=sl, r0=r0, valid=valid, sc=sc))
        for it in items:
            slope = 2.0 ** (-8.0 * (it["h"] + 1) / N_HEADS)
            sc = jnp.where(it["valid"], it["sc"] * scale - slope * absrel, NEG_INF)
            m = jnp.max(sc, axis=-1, keepdims=True)
            p = jnp.exp(sc - m)
            den = jnp.sum(p, axis=-1, keepdims=True)
            it["p"], it["den"], it["lse"] = p.astype(BF16), den, m + jnp.log(den)
        for it in items:
            pv = _dot(it["p"], window(vp_ref, vc_ref, vn_ref, it["gg"], it["j"], it["sl"]))
            o_scr[it["h"], it["r0"]:it["r0"] + tq, :] = pv / it["den"]
        for gg, j in group:
            lse_tile = jnp.zeros((tq, LANES), F32)
            for it in items:
                if (it["gg"], it["j"]) == (gg, j):
                    lse_tile = jnp.where(lane == it["h"], it["lse"], lse_tile)
            r0 = gg * w + j * tq
            l_scr[r0:r0 + tq, :] = lse_tile
    step = ATTN_PERM // dilation
    nblk = o_ref.shape[0]
    for blk in range(nblk):
        if step == 1:
            rows = slice(blk * half, (blk + 1) * half)
        else:
            rows = pl.ds(blk, half, stride=step)
        for h in range(N_HEADS):
            o_ref[blk, :, h * HEAD_DIM:(h + 1) * HEAD_DIM] = o_scr[h, rows, :].astype(o_ref.dtype)
        lse_ref[blk] = l_scr[rows, :]


def dilated_attention(proj, gi, dilation):
    b, s, _ = proj.shape
    d = D_MODEL
    half = ATTN_HALF
    tp = ATTN_TP
    assert s % tp == 0 and tp % (dilation * half) == 0 and ATTN_PERM % dilation == 0
    nt = s // tp
    w = tp // dilation
    g = 1 if w > half else ATTN_RES_PER_STEP
    nres = dilation // g
    base = gi * 3
    hpt = tp // half

    def cur(which):
        return pl.BlockSpec((None, g * w, d), lambda bi, t, r: (bi, t * nres + r, base + which))

    if g == 1:
        wh = w // half

        def prev(which):
            return pl.BlockSpec((None, half, d),
                                lambda bi, t, r: (bi, jnp.maximum((t - 1) * hpt + (r + 1) * wh - 1, 0), base + which))

        def nxt(which):
            return pl.BlockSpec((None, half, d),
                                lambda bi, t, r: (bi, jnp.minimum(t + 1, nt - 1) * hpt + r * wh, base + which))
    else:
        def prev(which):
            return pl.BlockSpec((None, g * w, d), lambda bi, t, r: (bi, jnp.maximum(t - 1, 0) * nres + r, base + which))

        def nxt(which):
            return pl.BlockSpec((None, g * w, d),
                                lambda bi, t, r: (bi, jnp.minimum(t + 1, nt - 1) * nres + r, base + which))

    perm = ATTN_PERM
    if dilation == perm:
        nblk, lead = g, (b, nt, perm)
        omap = lambda bi, t, r: (bi, t, r, 0, 0)
        oblock = lambda last: (None, None, nblk, half, last)
    else:
        nblk, lead = perm // dilation, (b, nt, perm // dilation, dilation)
        omap = lambda bi, t, r: (bi, t, 0, r, 0, 0)
        oblock = lambda last: (None, None, nblk, None, half, last)

    o, lse = pl.pallas_call(
        functools.partial(_attn_kernel, dilation=dilation, w=w, g=g, sub_len=s // dilation),
        grid=(b, nt, nres),
        in_specs=[cur(0), prev(1), cur(1), nxt(1), prev(2), cur(2), nxt(2)],
        out_specs=[pl.BlockSpec(oblock(d), omap), pl.BlockSpec(oblock(LANES), omap)],
        out_shape=[jax.ShapeDtypeStruct(lead + (half, d), BF16), jax.ShapeDtypeStruct(lead + (half, LANES), F32)],
        scratch_shapes=[pltpu.VMEM((N_HEADS, g * w, HEAD_DIM), F32), pltpu.VMEM((g * w, LANES), F32)],
        compiler_params=_cparams("parallel", "parallel", "parallel"),
        name=f"dilated_attn_g{gi}",
    )(proj, proj, proj, proj, proj, proj, proj)
    return o.reshape(b * s, d), lse.reshape(b * s, LANES)


def _attn_out_kernel(o0_ref, o1_ref, o2_ref, l0_ref, l1_ref, l2_ref, wo_ref, x_ref, out_ref, y_scr):
    parts = []
    for h in range(N_HEADS):
        sl = slice(h * HEAD_DIM, (h + 1) * HEAD_DIM)
        a0, a1, a2 = l0_ref[:, h:h + 1], l1_ref[:, h:h + 1], l2_ref[:, h:h + 1]
        mx = jnp.maximum(jnp.maximum(a0, a1), a2)
        e0, e1, e2 = jnp.exp(a0 - mx), jnp.exp(a1 - mx), jnp.exp(a2 - mx)
        o = (e0 * o0_ref[:, sl].astype(F32) + e1 * o1_ref[:, sl].astype(F32) + e2 * o2_ref[:, sl].astype(F32))
        parts.append((o / (e0 + e1 + e2)).astype(BF16))
    y = _dot(jnp.concatenate(parts, axis=1), wo_ref[...])
    ncol = y_scr.shape[0]
    for c in range(ncol):
        y_scr[c] = y[:, c * LANES:(c + 1) * LANES]
    per = ATTN_TP // ATTN_PERM
    for l in range(per):
        rows = slice(l * ATTN_PERM, (l + 1) * ATTN_PERM)
        for c in range(ncol):
            cols = slice(c * LANES, (c + 1) * LANES)
            out_ref[rows, cols] = x_ref[rows, cols] + y_scr[c, pl.ds(l, ATTN_PERM, stride=per), :]


def attn_out(os, lses, w_out, x):
    n, d = x.shape
    tm = ATTN_TP
    tile = lambda i: (i, 0)
    big = pl.BlockSpec((tm, d), tile)
    small = pl.BlockSpec((tm, LANES), tile)
    return pl.pallas_call(
        _attn_out_kernel,
        grid=(n // tm,),
        in_specs=[big, big, big, small, small, small, pl.BlockSpec((d, d), lambda i: (0, 0)), big],
        out_specs=big,
        out_shape=jax.ShapeDtypeStruct((n, d), F32),
        scratch_shapes=[pltpu.VMEM((d // LANES, tm, LANES), F32)],
        compiler_params=_cparams("parallel"),
        name="attn_out",
    )(*os, *lses, w_out, x)


def _attn_proj_kernel(x_ref, g_ref, w_ref, o_ref, h_ref, acc_ref, *, blocks_per_group):
    j = pl.program_id(1)

    @pl.when(j == 0)
    def _():
        h_ref[...] = _rms(x_ref[...], g_ref[...]).astype(BF16)

    res = _dot(h_ref[...], w_ref[...])
    ncol = acc_ref.shape[0]
    for c in range(ncol):
        acc_ref[c] = res[:, c * LANES:(c + 1) * LANES]
    gi = j // blocks_per_group
    for gidx, (_, dilation) in enumerate(ATTN_GROUPS):
        @pl.when(gi == gidx)
        def _():
            if dilation == 1:
                o_ref[...] = res.astype(o_ref.dtype)
            else:
                w = ATTN_TP // dilation
                for r in range(dilation):
                    for c in range(ncol):
                        o_ref[r * w:(r + 1) * w, c * LANES:(c + 1) * LANES] = (
                            acc_ref[c, pl.ds(r, w, stride=dilation), :].astype(o_ref.dtype))


def attn_proj(x, g, w, tn=512):
    n, d = x.shape
    nout = w.shape[1]
    tm = ATTN_TP
    group_cols = nout // len(ATTN_GROUPS)
    assert group_cols % tn == 0
    return pl.pallas_call(
        functools.partial(_attn_proj_kernel, blocks_per_group=group_cols // tn),
        grid=(n // tm, nout // tn),
        in_specs=[pl.BlockSpec((tm, d), lambda i, j: (i, 0)),
                  pl.BlockSpec((1, d), lambda i, j: (0, 0)),
                  pl.BlockSpec((d, tn), lambda i, j: (0, j))],
        out_specs=pl.BlockSpec((tm, tn), lambda i, j: (i, j)),
        out_shape=jax.ShapeDtypeStruct((n, nout), BF16),
        scratch_shapes=[pltpu.VMEM((tm, d), BF16), pltpu.VMEM((tn // LANES, tm, LANES), F32)],
        compiler_params=_cparams("parallel", "arbitrary"),
        name="attn_proj",
    )(x, g.reshape(1, d), w)


def _route_kernel(x_ref, g_ref, r_ref, h_ref, aff_ref):
    h = _rms(x_ref[...], g_ref[...])
    h_ref[...] = h.astype(BF16)
    logits = jnp.dot(h, r_ref[...], precision=lax.Precision.HIGHEST, preferred_element_type=F32)
    lt = logits.T[:N_EXPERTS, :]
    m = jnp.max(lt, axis=0, keepdims=True)
    e = jnp.exp(lt - m)
    aff_ref[...] = e / jnp.sum(e, axis=0, keepdims=True)


def moe_route(x, g, router, tm=512):
    n, d = x.shape
    rpad = jnp.zeros((d, LANES), F32).at[:, :N_EXPERTS].set(router)
    return pl.pallas_call(
        _route_kernel,
        grid=(n // tm,),
        in_specs=[pl.BlockSpec((tm, d), lambda i: (i, 0)),
                  pl.BlockSpec((1, d), lambda i: (0, 0)),
                  pl.BlockSpec((d, LANES), lambda i: (0, 0))],
        out_specs=[pl.BlockSpec((tm, d), lambda i: (i, 0)),
                   pl.BlockSpec((N_EXPERTS, tm), lambda i: (0, i))],
        out_shape=[jax.ShapeDtypeStruct((n, d), BF16), jax.ShapeDtypeStruct((N_EXPERTS, n), F32)],
        compiler_params=_cparams("parallel"),
        name="moe_route",
    )(x, g.reshape(1, d), rpad)


def _select_kernel(aff_ref, slot_ref, tab_ref, *, cap):
    e, n = aff_ref.shape
    bits = pltpu.bitcast(aff_ref[...], I32)
    tidx = lax.broadcasted_iota(I32, (e, n), 1)

    def count(mask_i32):
        return jnp.sum(mask_i32.astype(F32), axis=1, keepdims=True).astype(I32)

    def thr_body(it, thr):
        cand = thr | jnp.left_shift(jnp.int32(1), 30 - it)
        cnt = count(jnp.where(bits >= cand, 1, 0))
        return jnp.where(cnt >= cap, cand, thr)

    thr = lax.fori_loop(0, 31, thr_body, jnp.zeros((e, 1), I32))
    gt = jnp.where(bits > thr, 1, 0)
    eq = jnp.where(bits == thr, 1, 0)
    need = cap - count(gt)
    nbits = max(1, (n - 1).bit_length())

    def tie_body(it, j):
        cand = j | jnp.left_shift(jnp.int32(1), nbits - 1 - it)
        cnt = count(jnp.where(tidx < cand, eq, 0))
        return jnp.where(cnt < need, cand, j)

    jlast = lax.fori_loop(0, nbits, tie_body, jnp.zeros((e, 1), I32))
    sel = gt + jnp.where(tidx <= jlast, eq, 0)
    slot_ref[...] = sel
    rr = lax.broadcasted_iota(I32, (LANES, LANES), 0)
    cc = lax.broadcasted_iota(I32, (LANES, LANES), 1)
    upper = jnp.where(rr <= cc, 1.0, 0.0).astype(BF16)

    nblk = n // LANES
    blk_lane = lax.broadcasted_iota(I32, (e, nblk), 1)

    tab_ref[...] = jnp.zeros((e, nblk), I32)

    def blk_body(bi, run):
        off = pl.multiple_of(bi * LANES, LANES)
        sb = slot_ref[:, pl.ds(off, LANES)]
        inc = _dot(sb.astype(F32).astype(BF16), upper).astype(I32)
        slot_ref[:, pl.ds(off, LANES)] = jnp.where(sb > 0, inc - sb + run, -1)
        tab_ref[...] = jnp.where(blk_lane == bi, run, tab_ref[...])
        return run + inc[:, LANES - 1:LANES]

    lax.fori_loop(0, nblk, blk_body, jnp.zeros((e, 1), I32))


def moe_select(aff_t, cap):
    e, n = aff_t.shape
    return pl.pallas_call(
        functools.partial(_select_kernel, cap=cap),
        grid=(1,),
        in_specs=[pl.BlockSpec((e, n), lambda i: (0, 0))],
        out_specs=[pl.BlockSpec((e, n), lambda i: (0, 0)), pl.BlockSpec((e, n // LANES), lambda i: (0, 0))],
        out_shape=[jax.ShapeDtypeStruct((e, n), I32), jax.ShapeDtypeStruct((e, n // LANES), I32)],
        compiler_params=_cparams("arbitrary"),
        name="moe_select",
    )(aff_t)


def _ffn_kernel(x_ref, wg_ref, wu_ref, wd_ref, o_ref):
    x = x_ref[...]
    acc = None
    for f in range(EXPERT_FF // FF_CHUNK):
        fs = slice(f * FF_CHUNK, (f + 1) * FF_CHUNK)
        hid = (_silu(_dot(x, wg_ref[:, fs])) * _dot(x, wu_ref[:, fs])).astype(BF16)
        part = _dot(hid, wd_ref[fs, :])
        acc = part if acc is None else acc + part
    o_ref[...] = acc.astype(o_ref.dtype)


def moe_ffn(xe, w_gate, w_up, w_down, tm=512):
    e, c, d = xe.shape
    f = w_gate.shape[2]
    return pl.pallas_call(
        _ffn_kernel,
        grid=(e, c // tm),
        in_specs=[pl.BlockSpec((None, tm, d), lambda ei, j: (ei, j, 0)),
                  pl.BlockSpec((None, d, f), lambda ei, j: (ei, 0, 0)),
                  pl.BlockSpec((None, d, f), lambda ei, j: (ei, 0, 0)),
                  pl.BlockSpec((None, f, d), lambda ei, j: (ei, 0, 0))],
        out_specs=pl.BlockSpec((None, tm, d), lambda ei, j: (ei, j, 0)),
        out_shape=jax.ShapeDtypeStruct((e, c, d), BF16),
        compiler_params=_cparams("parallel", "arbitrary"),
        name="moe_ffn",
    )(xe, w_gate, w_up, w_down)


def _slot_ranges(tab_ref, i, cap):
    out = []
    for e in range(N_EXPERTS):
        p0 = tab_ref[i * N_EXPERTS + e]
        out.append((e * cap + p0, tab_ref[(i + 1) * N_EXPERTS + e] - p0))
    return out


def _dispatch_kernel(tab_ref, h_ref, slot_ref, xe_ref, stage_ref, tail_ref, sem, npiece_ref, *, cap, tile):
    i = pl.program_id(0)
    last = pl.num_programs(0) - 1
    slot = i % 2
    grp = BF16_SUBLANES
    kc = COMBINE_KCHUNK

    def drain(sl):
        def body(_, carry):
            pltpu.make_async_copy(stage_ref.at[sl, pl.ds(0, grp)], xe_ref.at[pl.ds(0, grp)], sem.at[sl]).wait()
            return carry
        lax.fori_loop(0, npiece_ref[sl], body, 0)

    @pl.when(i >= 2)
    def _():
        drain(slot)

    meta, bases = [], []
    off = jnp.int32(0)
    for g0, cnt in _slot_ranges(tab_ref, i, cap):
        a = (g0 // grp) * grp
        lead = g0 - a
        tot = lead + cnt
        meta.append((off, a, lead, tot, (tot // grp) * grp))
        bases.append(off - a)
        off = off + ((tot + grp - 1) // grp) * grp
    row_e = lax.broadcasted_iota(I32, (N_EXPERTS, 1), 0)
    cvec = jnp.zeros((N_EXPERTS, 1), I32)
    for e in range(N_EXPERTS):
        cvec = jnp.where(row_e == e, bases[e] + e * cap, cvec)
    slot_e = slot_ref[...]
    col = jnp.where(slot_e >= 0, slot_e + cvec, -1)

    def chunk_body(ci, carry):
        base = pl.multiple_of(ci * kc, kc)
        rows = lax.broadcasted_iota(I32, (kc, tile), 0) + base
        p = jnp.zeros((kc, tile), F32)
        for e in range(N_EXPERTS):
            p = jnp.where(col[e:e + 1, :] == rows, 1.0, p)
        stage_ref[slot, pl.ds(base, kc), :] = _dot(p.astype(BF16), h_ref[...]).astype(stage_ref.dtype)
        return carry

    lax.fori_loop(0, (off + kc - 1) // kc, chunk_body, 0)

    row_g = lax.broadcasted_iota(I32, (grp, D_MODEL), 0)
    npiece = jnp.int32(0)
    for e, (off_e, a, lead, tot, nfull) in enumerate(meta):
        first = pl.ds(pl.multiple_of(off_e, grp), grp)

        @pl.when(lead > 0)
        def _():
            stage_ref[slot, first, :] = jnp.where(row_g < lead, tail_ref[e], stage_ref[slot, first, :])

        @pl.when(tot > nfull)
        def _():
            tail_ref[e] = stage_ref[slot, pl.ds(pl.multiple_of(off_e + nfull, grp), grp), :]

        def piece(k, carry):
            pltpu.make_async_copy(stage_ref.at[slot, pl.ds(pl.multiple_of(off_e + k * grp, grp), grp)],
                                  xe_ref.at[pl.ds(pl.multiple_of(a + k * grp, grp), grp)], sem.at[slot]).start()
            return carry

        lax.fori_loop(0, nfull // grp, piece, 0)
        npiece = npiece + nfull // grp
    npiece_ref[slot] = npiece

    @pl.when(i == last)
    def _():
        drain(slot)

        @pl.when(i >= 1)
        def _():
            drain(1 - slot)


def moe_dispatch(hb, slot, tab, cap, tile=COMBINE_TILE):
    n, d = hb.shape
    kmax = N_EXPERTS * (tile + BF16_SUBLANES)
    kmax = -(-kmax // COMBINE_KCHUNK) * COMBINE_KCHUNK
    grid_spec = pltpu.PrefetchScalarGridSpec(
        num_scalar_prefetch=1,
        grid=(n // tile,),
        in_specs=[pl.BlockSpec((tile, d), lambda i, tab: (i, 0)),
                  pl.BlockSpec((N_EXPERTS, tile), lambda i, tab: (0, i))],
        out_specs=pl.BlockSpec(memory_space=pl.ANY),
        scratch_shapes=[pltpu.VMEM((2, kmax, d), BF16), pltpu.VMEM((N_EXPERTS, BF16_SUBLANES, d), BF16),
                        pltpu.SemaphoreType.DMA((2,)), pltpu.SMEM((2,), I32)],
    )
    return pl.pallas_call(
        functools.partial(_dispatch_kernel, cap=cap, tile=tile),
        grid_spec=grid_spec,
        out_shape=jax.ShapeDtypeStruct((N_EXPERTS * cap, d), BF16),
        compiler_params=_cparams("arbitrary"),
        name="moe_dispatch",
    )(tab, hb, slot)


def _combine_ranges(tab_ref, i, cap):
    grp = BF16_SUBLANES
    slabs, bases = [], []
    off = jnp.int32(0)
    for e, (g0, cnt) in enumerate(_slot_ranges(tab_ref, i, cap)):
        a = (g0 // grp) * grp
        nrow = jnp.where(cnt > 0, ((g0 - a + cnt + grp - 1) // grp) * grp, 0)
        slabs.append((off, a, nrow))
        bases.append(off - a + e * cap)
        off = off + nrow
    return slabs, bases, off


def _combine_piece(ye_ref, zbuf_ref, sem, src, dst):
    grp = BF16_SUBLANES
    return pltpu.make_async_copy(ye_ref.at[pl.ds(pl.multiple_of(src, grp), grp)],
                                 zbuf_ref.at[pl.ds(pl.multiple_of(dst, grp), grp)], sem)


def _combine_kernel(tab_ref, x_ref, slot_ref, aff_ref, ye_ref, g_ref, o_ref, zbuf_ref, sem,
                    *, cap, tile, final_norm):
    i = pl.program_id(0)
    slot = i % 2

    grp = BF16_SUBLANES

    def start_tile(ti, sl):
        for off_e, a, nrow in _combine_ranges(tab_ref, ti, cap)[0]:
            def piece(k, carry):
                _combine_piece(ye_ref, zbuf_ref.at[sl], sem.at[sl], a + k * grp, off_e + k * grp).start()
                return carry
            lax.fori_loop(0, nrow // grp, piece, 0)

    @pl.when(i == 0)
    def _():
        zbuf_ref[...] = jnp.zeros_like(zbuf_ref)
        start_tile(0, 0)

    @pl.when(i + 1 < pl.num_programs(0))
    def _():
        start_tile(i + 1, 1 - slot)

    zcur_ref = zbuf_ref.at[slot]
    _, bases, k_tot = _combine_ranges(tab_ref, i, cap)
    lane_e = lax.broadcasted_iota(I32, (1, N_EXPERTS), 1)
    cvec = jnp.zeros((1, N_EXPERTS), I32)
    for e in range(N_EXPERTS):
        cvec = jnp.where(lane_e == e, bases[e], cvec)
    slot_t = slot_ref[...]
    col = jnp.where(slot_t >= 0, slot_t + cvec, -1)
    aff = aff_ref[...]
    o_ref[...] = x_ref[...]

    def wait_piece(_, carry):
        _combine_piece(ye_ref, zcur_ref, sem.at[slot], 0, 0).wait()
        return carry

    lax.fori_loop(0, k_tot // grp, wait_piece, 0)
    kc = COMBINE_KCHUNK

    def chunk_body(ci, carry):
        base = pl.multiple_of(ci * kc, kc)
        z = zcur_ref[pl.ds(base, kc), :]
        lanes = lax.broadcasted_iota(I32, (tile, kc), 1) + base
        p = jnp.zeros((tile, kc), F32)
        for e in range(N_EXPERTS):
            p = jnp.where(col[:, e:e + 1] == lanes, aff[:, e:e + 1], p)
        p_hi = p.astype(BF16)
        p_lo = (p - p_hi.astype(F32)).astype(BF16)
        o_ref[...] += _dot(p_hi, z) + _dot(p_lo, z)
        return carry

    lax.fori_loop(0, (k_tot + kc - 1) // kc, chunk_body, 0)
    if final_norm:
        o_ref[...] = _rms(o_ref[...], g_ref[...])


def moe_combine(x, slot_t, aff, tab, ye, cap, norm_g=None, tile=COMBINE_TILE):
    n, d = x.shape
    kmax = N_EXPERTS * (tile + BF16_SUBLANES)
    kmax = -(-kmax // COMBINE_KCHUNK) * COMBINE_KCHUNK
    final_norm = norm_g is not None
    g = (norm_g if final_norm else jnp.ones((d,), F32)).reshape(1, d)
    grid_spec = pltpu.PrefetchScalarGridSpec(
        num_scalar_prefetch=1,
        grid=(n // tile,),
        in_specs=[pl.BlockSpec((tile, d), lambda i, tab: (i, 0)),
                  pl.BlockSpec((tile, N_EXPERTS), lambda i, tab: (i, 0)),
                  pl.BlockSpec((tile, N_EXPERTS), lambda i, tab: (i, 0)),
                  pl.BlockSpec(memory_space=pl.ANY),
                  pl.BlockSpec((1, d), lambda i, tab: (0, 0))],
        out_specs=pl.BlockSpec((tile, d), lambda i, tab: (i, 0)),
        scratch_shapes=[pltpu.VMEM((2, kmax, d), BF16), pltpu.SemaphoreType.DMA((2,))],
    )
    return pl.pallas_call(
        functools.partial(_combine_kernel, cap=cap, tile=tile, final_norm=final_norm),
        grid_spec=grid_spec,
        out_shape=jax.ShapeDtypeStruct((n, d), F32),
        compiler_params=_cparams("arbitrary"),
        name="moe_combine",
    )(tab, x, slot_t, aff, ye, g)


def expert_choice_moe(x, norm_g, router, w_gate, w_up, w_down, final_g=None):
    n, d = x.shape
    cap = max(1, CAPACITY_FACTOR * n // N_EXPERTS)
    hb, aff_t = moe_route(x, norm_g, router)
    slot, blk_tab = moe_select(aff_t, cap)
    tab = jnp.concatenate([blk_tab[:, ::COMBINE_TILE // LANES].T, jnp.full((1, N_EXPERTS), cap, I32)], axis=0)
    tab = tab.reshape(-1)
    xe = moe_dispatch(hb, slot, tab, cap).reshape(N_EXPERTS, cap, d)
    ye = moe_ffn(xe, w_gate, w_up, w_down).reshape(N_EXPERTS * cap, d)
    return moe_combine(x, slot.T, aff_t.T, tab, ye, cap, norm_g=final_g)


def _trunk(x, p):
    b, s, d = x.shape
    n = b * s
    x = x.reshape(n, d)
    for i in range(DEPTH):
        kind, j = i % N_MIXERS, i // N_MIXERS
        if kind == 0:
            u = norm_glu(x, p["norm_mix"][i], p["conv_w_glu"][j], p["conv_b_glu"][j])
            x = conv_out(u.reshape(b, s, d), x.reshape(b, s, d), p["conv_w_dw"][j], p["conv_b_dw"][j],
                         p["conv_ln_g"][j], p["conv_ln_b"][j], p["conv_w_out"][j], p["conv_b_out"][j]).reshape(n, d)
        elif kind == 1:
            proj = norm_linear(x, p["norm_mix"][i], p["dn_w_in"][j], F32)
            q, k, v, bg = dn_prep(proj.reshape(b, s, -1), p["dn_w_conv"][j], p["dn_a_log"][j], p["dn_dt_bias"][j])
            o_f, o_b = delta_rule(q, k, v, bg)
            x = dn_out(o_f.reshape(n, d), o_b.reshape(n, d), proj, p["dn_norm_g"][j], p["dn_w_out"][j], x)
        else:
            proj = attn_proj(x, p["norm_mix"][i], p["attn_w_in"][j]).reshape(b, s, -1)
            os, lses = [], []
            for gi, (window, dilation) in enumerate(ATTN_GROUPS):
                assert window // (2 * dilation) == ATTN_HALF
                o, lse = dilated_attention(proj, gi, dilation)
                os.append(o)
                lses.append(lse)
            x = attn_out(os, lses, p["attn_w_out"][j], x)
        x = expert_choice_moe(x, p["norm_ffn"][i], p["moe_router"][i], p["moe_w_gate"][i], p["moe_w_up"][i],
                              p["moe_w_down"][i], final_g=p["norm_final"] if i == DEPTH - 1 else None)
    return x.reshape(b, s, d)


def kernel(x_prompt, x_sample, norm_mix, norm_ffn, norm_final, conv_w_glu, conv_b_glu, conv_w_dw, conv_b_dw,
           conv_ln_g, conv_ln_b, conv_w_out, conv_b_out, dn_w_in, dn_w_conv, dn_a_log, dn_dt_bias, dn_norm_g,
           dn_w_out, attn_w_in, attn_w_out, moe_router, moe_w_gate, moe_w_up, moe_w_down):
    dn_pad = (-dn_w_in.shape[-1]) % 512
    p = dict(
        norm_mix=norm_mix, norm_ffn=norm_ffn, norm_final=norm_final,
        conv_w_glu=conv_w_glu.astype(BF16), conv_b_glu=conv_b_glu, conv_w_dw=conv_w_dw, conv_b_dw=conv_b_dw,
        conv_ln_g=conv_ln_g, conv_ln_b=conv_ln_b, conv_w_out=conv_w_out.astype(BF16), conv_b_out=conv_b_out,
        dn_w_in=jnp.pad(dn_w_in, ((0, 0), (0, 0), (0, dn_pad))).astype(BF16), dn_w_conv=dn_w_conv,
        dn_a_log=dn_a_log, dn_dt_bias=dn_dt_bias, dn_norm_g=dn_norm_g, dn_w_out=dn_w_out.astype(BF16),
        attn_w_in=attn_w_in.astype(BF16), attn_w_out=attn_w_out.astype(BF16),
        moe_router=moe_router, moe_w_gate=moe_w_gate.astype(BF16), moe_w_up=moe_w_up.astype(BF16),
        moe_w_down=moe_w_down.astype(BF16),
    )
    return (_trunk(x_prompt, p), _trunk(x_sample, p))
```

```python
import functools
import math

import jax
import jax.numpy as jnp
from jax import lax
from jax.experimental import pallas as pl
from jax.experimental.pallas import tpu as pltpu

F32 = jnp.float32
BF16 = jnp.bfloat16
I32 = jnp.int32

D_MODEL = 1024
DEPTH = 4
N_MIXERS = 3
RMS_EPS = 1e-6
LN_EPS = 1e-5
NEG_INF = -1e30
CONV_WIDTH = 31
HEAD_DIM = 128
N_HEADS = D_MODEL // HEAD_DIM
SHORT_CONV = 5
ATTN_GROUPS = ((128, 1), (512, 4), (2048, 16))
N_EXPERTS = 16
EXPERT_FF = 2 * D_MODEL
CAPACITY_FACTOR = 2

VMEM_LIMIT_BYTES = 56 * 1024 * 1024
LANES = 128
BF16_SUBLANES = 16

DN_CHUNK = 64
DN_SPLIT_PASSES = False
ATTN_HALF = 64
ATTN_TQ = 128
ATTN_PERM = 16
ATTN_TP = ATTN_PERM * ATTN_HALF
ATTN_RES_PER_STEP = 4
FF_CHUNK = 512
COMBINE_TILE = 256
COMBINE_KCHUNK = 256


def _cparams(*sem):
    return pltpu.CompilerParams(dimension_semantics=sem, vmem_limit_bytes=VMEM_LIMIT_BYTES)


def _rms(x, g):
    return x * lax.rsqrt(jnp.mean(x * x, axis=-1, keepdims=True) + RMS_EPS) * g


def _silu(x):
    return x * jax.nn.sigmoid(x)


def _dot(a, b):
    return jnp.dot(a, b, preferred_element_type=F32)


def _dot_nt(a, b):
    return lax.dot_general(a, b, (((1,), (1,)), ((), ())), preferred_element_type=F32)


def _dot_tn(a, b):
    return lax.dot_general(a, b, (((0,), (0,)), ((), ())), preferred_element_type=F32)


def _norm_linear_kernel(x_ref, g_ref, w_ref, o_ref, h_ref):
    @pl.when(pl.program_id(1) == 0)
    def _():
        h_ref[...] = _rms(x_ref[...], g_ref[...]).astype(BF16)

    o_ref[...] = _dot(h_ref[...], w_ref[...]).astype(o_ref.dtype)


def norm_linear(x, g, w, out_dtype, tm=1024, tn=512):
    n, d = x.shape
    nout = w.shape[1]
    return pl.pallas_call(
        _norm_linear_kernel,
        grid=(n // tm, nout // tn),
        in_specs=[pl.BlockSpec((tm, d), lambda i, j: (i, 0)),
                  pl.BlockSpec((1, d), lambda i, j: (0, 0)),
                  pl.BlockSpec((d, tn), lambda i, j: (0, j))],
        out_specs=pl.BlockSpec((tm, tn), lambda i, j: (i, j)),
        out_shape=jax.ShapeDtypeStruct((n, nout), out_dtype),
        scratch_shapes=[pltpu.VMEM((tm, d), BF16)],
        compiler_params=_cparams("parallel", "arbitrary"),
        name="norm_linear",
    )(x, g.reshape(1, d), w)


def _norm_glu_kernel(x_ref, g_ref, wa_ref, wg_ref, ba_ref, bg_ref, o_ref, h_ref):
    @pl.when(pl.program_id(1) == 0)
    def _():
        h_ref[...] = _rms(x_ref[...], g_ref[...]).astype(BF16)

    h = h_ref[...]
    a = _dot(h, wa_ref[...]) + ba_ref[...]
    gate = _dot(h, wg_ref[...]) + bg_ref[...]
    o_ref[...] = a * jax.nn.sigmoid(gate)


def norm_glu(x, g, w, b, tm=1024, tn=1024):
    n, d = x.shape
    nb = d // tn
    b2 = b.reshape(1, 2 * d)
    return pl.pallas_call(
        _norm_glu_kernel,
        grid=(n // tm, nb),
        in_specs=[pl.BlockSpec((tm, d), lambda i, j: (i, 0)),
                  pl.BlockSpec((1, d), lambda i, j: (0, 0)),
                  pl.BlockSpec((d, tn), lambda i, j: (0, j)),
                  pl.BlockSpec((d, tn), lambda i, j: (0, j + nb)),
                  pl.BlockSpec((1, tn), lambda i, j: (0, j)),
                  pl.BlockSpec((1, tn), lambda i, j: (0, j + nb))],
        out_specs=pl.BlockSpec((tm, tn), lambda i, j: (i, j)),
        out_shape=jax.ShapeDtypeStruct((n, d), F32),
        scratch_shapes=[pltpu.VMEM((tm, d), BF16)],
        compiler_params=_cparams("parallel", "arbitrary"),
        name="norm_glu",
    )(x, g.reshape(1, d), w, w, b2, b2)


CONV_HALO = 16


def _conv_out_kernel(up_ref, uc_ref, un_ref, wdw_ref, bdw_ref, lng_ref, lnb_ref, wo_ref, bo_ref, x_ref,
                     o_ref, win_ref, shift_ref, *, ts):
    i = pl.program_id(1)
    last = pl.num_programs(1) - 1
    win_ref[0:CONV_HALO, :] = jnp.where(i > 0, up_ref[...], 0.0)
    win_ref[CONV_HALO:CONV_HALO + ts, :] = uc_ref[...]
    win_ref[CONV_HALO + ts:, :] = jnp.where(i < last, un_ref[...], 0.0)
    for s in range(1, 8):
        shift_ref[s - 1] = win_ref[pl.ds(s, ts + 24), :]
    acc = jnp.zeros((ts, D_MODEL), F32) + bdw_ref[...]
    for s in range(8):
        for a in range(4):
            j = 8 * a + s - 1
            if 0 <= j < CONV_WIDTH:
                rows = slice(8 * a, 8 * a + ts)
                tap = win_ref[rows, :] if s == 0 else shift_ref[s - 1, rows, :]
                acc = acc + wdw_ref[j:j + 1, :] * tap
    mu = jnp.mean(acc, axis=-1, keepdims=True)
    cen = acc - mu
    var = jnp.mean(cen * cen, axis=-1, keepdims=True)
    u = cen * lax.rsqrt(var + LN_EPS) * lng_ref[...] + lnb_ref[...]
    y = _silu(u).astype(BF16)
    o_ref[...] = x_ref[...] + _dot(y, wo_ref[...]) + bo_ref[...]


def conv_out(u, x, w_dw, b_dw, ln_g, ln_b, w_out, b_out, ts=256):
    b, s, d = u.shape
    hb = ts // CONV_HALO
    nblk = s // CONV_HALO
    wdw = jnp.concatenate([w_dw, jnp.zeros((1, d), F32)], axis=0)
    row = lambda v: v.reshape(1, d)
    return pl.pallas_call(
        functools.partial(_conv_out_kernel, ts=ts),
        grid=(b, s // ts),
        in_specs=[pl.BlockSpec((None, CONV_HALO, d), lambda bi, i: (bi, jnp.maximum(i * hb - 1, 0), 0)),
                  pl.BlockSpec((None, ts, d), lambda bi, i: (bi, i, 0)),
                  pl.BlockSpec((None, CONV_HALO, d), lambda bi, i: (bi, jnp.minimum((i + 1) * hb, nblk - 1), 0)),
                  pl.BlockSpec((CONV_WIDTH + 1, d), lambda bi, i: (0, 0)),
                  pl.BlockSpec((1, d), lambda bi, i: (0, 0)),
                  pl.BlockSpec((1, d), lambda bi, i: (0, 0)),
                  pl.BlockSpec((1, d), lambda bi, i: (0, 0)),
                  pl.BlockSpec((d, d), lambda bi, i: (0, 0)),
                  pl.BlockSpec((1, d), lambda bi, i: (0, 0)),
                  pl.BlockSpec((None, ts, d), lambda bi, i: (bi, i, 0))],
        out_specs=pl.BlockSpec((None, ts, d), lambda bi, i: (bi, i, 0)),
        out_shape=jax.ShapeDtypeStruct((b, s, d), F32),
        scratch_shapes=[pltpu.VMEM((ts + 2 * CONV_HALO, d), F32), pltpu.VMEM((7, ts + 24, d), F32)],
        compiler_params=_cparams("parallel", "parallel"),
        name="conv_out",
    )(u, u, u, wdw, row(b_dw), row(ln_g), row(ln_b), w_out, row(b_out), x)


DN_HALO = 8
DN_BG = 32


def _dn_prep_kernel(pp_ref, pc_ref, pn_ref, ba_ref, wc_ref, alog_ref, dtb_ref,
                    q_ref, k_ref, v_ref, bg_ref, win_ref, *, ts):
    i = pl.program_id(1)
    last = pl.num_programs(1) - 1
    w3 = 3 * D_MODEL
    win_ref[0:DN_HALO, :] = jnp.where(i > 0, pp_ref[...], 0.0)
    win_ref[DN_HALO:DN_HALO + ts, :] = pc_ref[...]
    win_ref[DN_HALO + ts:, :] = jnp.where(i < last, pn_ref[...], 0.0)
    left = (SHORT_CONV - 1) // 2
    acc = jnp.zeros((ts, w3), F32)
    for j in range(SHORT_CONV):
        acc = acc + wc_ref[j:j + 1, :] * win_ref[pl.ds(DN_HALO - left + j, ts), :]
    qkv = _silu(acc)

    def l2n(t):
        return t * lax.rsqrt(jnp.sum(t * t, axis=-1, keepdims=True) + RMS_EPS)

    for h in range(N_HEADS):
        sl = slice(h * HEAD_DIM, (h + 1) * HEAD_DIM)
        q_ref[:, sl] = l2n(qkv[:, sl]) * (HEAD_DIM ** -0.5)
        k_ref[:, sl] = l2n(qkv[:, D_MODEL + h * HEAD_DIM:D_MODEL + (h + 1) * HEAD_DIM])
    v_ref[...] = qkv[:, 2 * D_MODEL:]
    ba = ba_ref[:, :DN_BG]
    lane = lax.broadcasted_iota(I32, (ts, DN_BG), 1)
    beta = jax.nn.sigmoid(ba)
    g = -jnp.exp(alog_ref[...]) * jax.nn.softplus(ba + dtb_ref[...])
    bg_ref[...] = jnp.where(lane < DN_BG // 2, beta, g)


def dn_prep(proj, w_conv, a_log, dt_bias, ts=256):
    b, s, _ = proj.shape
    d = D_MODEL
    hb = ts // DN_HALO
    nblk = s // DN_HALO
    zeros16 = jnp.zeros((16,), F32)
    alog = jnp.concatenate([zeros16, a_log.reshape(-1)]).reshape(1, DN_BG)
    dtb = jnp.concatenate([zeros16, dt_bias.reshape(-1)]).reshape(1, DN_BG)
    bacol = 4 * d // LANES
    tile = lambda bi, i: (bi, i, 0)
    return pl.pallas_call(
        functools.partial(_dn_prep_kernel, ts=ts),
        grid=(b, s // ts),
        in_specs=[pl.BlockSpec((None, DN_HALO, 3 * d), lambda bi, i: (bi, jnp.maximum(i * hb - 1, 0), 0)),
                  pl.BlockSpec((None, ts, 3 * d), tile),
                  pl.BlockSpec((None, DN_HALO, 3 * d), lambda bi, i: (bi, jnp.minimum((i + 1) * hb, nblk - 1), 0)),
                  pl.BlockSpec((None, ts, LANES), lambda bi, i: (bi, i, bacol)),
                  pl.BlockSpec((SHORT_CONV, 3 * d), lambda bi, i: (0, 0)),
                  pl.BlockSpec((1, DN_BG), lambda bi, i: (0, 0)),
                  pl.BlockSpec((1, DN_BG), lambda bi, i: (0, 0))],
        out_specs=[pl.BlockSpec((None, ts, d), tile),
                   pl.BlockSpec((None, ts, d), tile),
                   pl.BlockSpec((None, ts, d), tile),
                   pl.BlockSpec((None, ts, DN_BG), tile)],
        out_shape=[jax.ShapeDtypeStruct((b, s, d), F32)] * 3 + [jax.ShapeDtypeStruct((b, s, DN_BG), F32)],
        scratch_shapes=[pltpu.VMEM((ts + 2 * DN_HALO, 3 * d), F32)],
        compiler_params=_cparams("parallel", "parallel"),
        name="dn_prep",
    )(proj, proj, proj, proj, w_conv, alog, dtb)


def _delta_kernel(qf_ref, kf_ref, vf_ref, bgf_ref, qb_ref, kb_ref, vb_ref, bgb_ref,
                  of_ref, ob_ref, state_ref, *, chunk):
    @pl.when(pl.program_id(1) == 0)
    def _():
        state_ref[...] = jnp.zeros_like(state_ref)

    c = chunk
    dh = HEAD_DIM
    ns = 2 * dh
    r = lax.broadcasted_iota(I32, (c, c), 0)
    s = lax.broadcasted_iota(I32, (c, c), 1)
    eye = jnp.where(lax.broadcasted_iota(I32, (DN_BG, DN_BG), 0) == lax.broadcasted_iota(I32, (DN_BG, DN_BG), 1),
                    1.0, 0.0)
    hi = lax.Precision.HIGHEST
    chains = []
    for di, (q_ref, k_ref, v_ref, bg_ref, o_ref) in enumerate(
            ((qf_ref, kf_ref, vf_ref, bgf_ref, of_ref), (qb_ref, kb_ref, vb_ref, bgb_ref, ob_ref))):
        reverse = di == 1
        incl = (r <= s) if reverse else (r >= s)
        strict = (r < s) if reverse else (r > s)
        bg = bg_ref[...]
        gc_cols = jnp.dot(jnp.where(incl, 1.0, 0.0), bg, precision=hi, preferred_element_type=F32)
        gc_rows = lax.dot_general(eye, gc_cols, (((1,), (1,)), ((), ())),
                                  precision=hi, preferred_element_type=F32)
        for h in range(N_HEADS):
            jb = di * N_HEADS + h
            jg = DN_BG // 2 + jb
            gc_col = gc_cols[:, jg:jg + 1]
            chains.append(dict(
                sl=slice(h * dh, (h + 1) * dh), jb=jb, q_ref=q_ref, k_ref=k_ref, v_ref=v_ref, o_ref=o_ref,
                incl=incl, strict=strict, beta=bg[:, jb:jb + 1], gc_col=gc_col, gc_row=gc_rows[jg:jg + 1, :],
                g_tot=gc_col[0:1, :] if reverse else gc_col[c - 1:c, :]))

    for ch in chains:
        q, k, v = ch["q_ref"][:, ch["sl"]], ch["k_ref"][:, ch["sl"]], ch["v_ref"][:, ch["sl"]]
        decay = jnp.exp(jnp.where(ch["incl"], ch["gc_col"] - ch["gc_row"], NEG_INF))
        kb = k * ch["beta"]
        egc = jnp.exp(ch["gc_col"])
        qk_kk = _dot_nt(jnp.concatenate([q, kb], axis=0).astype(BF16), k.astype(BF16))
        ch["a"] = jnp.where(ch["incl"], qk_kk[:c] * decay, 0.0).astype(BF16)
        m = jnp.where(ch["strict"], -qk_kk[c:] * decay, 0.0)
        ch["x"] = jnp.concatenate([v * ch["beta"], kb * egc, m], axis=1)
        ch["qd"] = (q * egc).astype(BF16)
        ch["kd"] = (k * jnp.exp(ch["g_tot"] - ch["gc_col"])).astype(BF16)

    n_levels = max(1, (c - 1).bit_length())
    for lvl in range(n_levels):
        width = ns + c if lvl + 1 < n_levels else ns
        for ch in chains:
            x = ch["x"]
            xh = x.astype(BF16)
            if DN_SPLIT_PASSES:
                xl = (x - xh.astype(F32)).astype(BF16)
                lhs = jnp.concatenate([xh[:, ns:], xh[:, ns:], xl[:, ns:]], axis=1)
                rhs = jnp.concatenate([xh[:, :width], xl[:, :width], xh[:, :width]], axis=0)
            else:
                lhs, rhs = xh[:, ns:], xh[:, :width]
            prod = _dot(lhs, rhs)
            sol = x[:, :ns] + prod[:, :ns]
            ch["x"] = jnp.concatenate([sol, prod[:, ns:]], axis=1) if width > ns else sol

    for ch in chains:
        ch["state"] = state_ref[ch["jb"]]
        lhs = jnp.concatenate([ch["x"][:, dh:].astype(BF16), ch["qd"]], axis=0)
        ch["ws"] = _dot(lhs, ch["state"].astype(BF16))
    for ch in chains:
        ch["vb"] = (ch["x"][:, :dh] - ch["ws"][:c]).astype(BF16)
        ch["o_ref"][:, ch["sl"]] = ch["ws"][c:] + _dot(ch["a"], ch["vb"])
    for ch in chains:
        state_ref[ch["jb"]] = ch["state"] * jnp.exp(ch["g_tot"]) + _dot_tn(ch["kd"], ch["vb"])


def delta_rule(q, k, v, bg, chunk=DN_CHUNK):
    b, s, d = q.shape
    nc = s // chunk
    fwd = lambda bi, c: (bi, c, 0)
    bwd = lambda bi, c: (bi, nc - 1 - c, 0)
    big = lambda im: pl.BlockSpec((None, chunk, d), im)
    return pl.pallas_call(
        functools.partial(_delta_kernel, chunk=chunk),
        grid=(b, nc),
        in_specs=[big(fwd), big(fwd), big(fwd), pl.BlockSpec((None, chunk, DN_BG), fwd),
                  big(bwd), big(bwd), big(bwd), pl.BlockSpec((None, chunk, DN_BG), bwd)],
        out_specs=[big(fwd), big(bwd)],
        out_shape=[jax.ShapeDtypeStruct((b, s, d), F32)] * 2,
        scratch_shapes=[pltpu.VMEM((2 * N_HEADS, HEAD_DIM, HEAD_DIM), F32)],
        compiler_params=_cparams("parallel", "arbitrary"),
        name="delta_rule",
    )(q, k, v, bg, q, k, v, bg)


def _dn_out_kernel(of_ref, ob_ref, z_ref, ng_ref, wo_ref, x_ref, o_ref):
    parts = []
    for h in range(N_HEADS):
        sl = slice(h * HEAD_DIM, (h + 1) * HEAD_DIM)
        o = of_ref[:, sl] + ob_ref[:, sl]
        o = o * lax.rsqrt(jnp.mean(o * o, axis=-1, keepdims=True) + RMS_EPS) * ng_ref[...] * _silu(z_ref[:, sl])
        parts.append(o.astype(BF16))
    o_ref[...] = x_ref[...] + _dot(jnp.concatenate(parts, axis=1), wo_ref[...])


def dn_out(o_f, o_b, proj, norm_g, w_out, x, tm=512):
    n, d = x.shape
    tile = lambda i: (i, 0)
    return pl.pallas_call(
        _dn_out_kernel,
        grid=(n // tm,),
        in_specs=[pl.BlockSpec((tm, d), tile), pl.BlockSpec((tm, d), tile),
                  pl.BlockSpec((tm, d), lambda i: (i, 3)),
                  pl.BlockSpec((1, HEAD_DIM), lambda i: (0, 0)),
                  pl.BlockSpec((d, d), lambda i: (0, 0)),
                  pl.BlockSpec((tm, d), tile)],
        out_specs=pl.BlockSpec((tm, d), tile),
        out_shape=jax.ShapeDtypeStruct((n, d), F32),
        compiler_params=_cparams("parallel"),
        name="dn_out",
    )(o_f, o_b, proj, norm_g.reshape(1, HEAD_DIM), w_out, x)


def _attn_kernel(q_ref, kp_ref, kc_ref, kn_ref, vp_ref, vc_ref, vn_ref, o_ref, lse_ref, o_scr, l_scr,
                 *, dilation, w, g, sub_len):
    t = pl.program_id(1)
    half = ATTN_HALF
    tq = min(ATTN_TQ, w)
    nsub = w // tq
    tk = tq + 2 * half
    row = lax.broadcasted_iota(I32, (tq, tk), 0)
    col = lax.broadcasted_iota(I32, (tq, tk), 1)
    rel = col - half - row
    band = jnp.abs(rel) <= half
    absrel = jnp.abs(rel).astype(F32) * float(dilation)
    lane = lax.broadcasted_iota(I32, (tq, LANES), 1)
    scale = HEAD_DIM ** -0.5

    def window(p_ref, c_ref, n_ref, gg, j, sl):
        r0 = gg * w + j * tq
        hrow = slice(gg * half, (gg + 1) * half) if g > 1 else slice(None)
        before = p_ref[hrow, sl] if j == 0 else c_ref[r0 - half:r0, sl]
        after = n_ref[hrow, sl] if j == nsub - 1 else c_ref[r0 + tq:r0 + tq + half, sl]
        return jnp.concatenate([before, c_ref[r0:r0 + tq, sl], after], axis=0)

    blocks = [(gg, j) for gg in range(g) for j in range(nsub)]
    groups = [blocks] if g > 1 else [[blk] for blk in blocks]
    for group in groups:
        items = []
        for gg, j in group:
            r0 = gg * w + j * tq
            key = t * w + j * tq - half + col
            valid = band & (key >= 0) & (key < sub_len)
            for h in range(N_HEADS):
                sl = slice(h * HEAD_DIM, (h + 1) * HEAD_DIM)
                sc = _dot_nt(q_ref[r0:r0 + tq, sl], window(kp_ref, kc_ref, kn_ref, gg, j, sl))
                items.append(dict(gg=gg, j=j, h=h, sl=sl, r0=r0, valid=valid, sc=sc))
        for it in items:
            slope = 2.0 ** (-8.0 * (it["h"] + 1) / N_HEADS)
            sc = jnp.where(it["valid"], it["sc"] * scale - slope * absrel, NEG_INF)
            m = jnp.max(sc, axis=-1, keepdims=True)
            p = jnp.exp(sc - m)
            den = jnp.sum(p, axis=-1, keepdims=True)
            it["p"], it["den"], it["lse"] = p.astype(BF16), den, m + jnp.log(den)
        for it in items:
            pv = _dot(it["p"], window(vp_ref, vc_ref, vn_ref, it["gg"], it["j"], it["sl"]))
            o_scr[it["h"], it["r0"]:it["r0"] + tq, :] = pv / it["den"]
        for gg, j in group:
            lse_tile = jnp.zeros((tq, LANES), F32)
            for it in items:
                if (it["gg"], it["j"]) == (gg, j):
                    lse_tile = jnp.where(lane == it["h"], it["lse"], lse_tile)
            r0 = gg * w + j * tq
            l_scr[r0:r0 + tq, :] = lse_tile
    step = ATTN_PERM // dilation
    nblk = o_ref.shape[0]
    for blk in range(nblk):
        if step == 1:
            rows = slice(blk * half, (blk + 1) * half)
        else:
            rows = pl.ds(blk, half, stride=step)
        for h in range(N_HEADS):
            o_ref[blk, :, h * HEAD_DIM:(h + 1) * HEAD_DIM] = o_scr[h, rows, :].astype(o_ref.dtype)
        lse_ref[blk] = l_scr[rows, :]


def dilated_attention(proj, gi, dilation):
    b, s, _ = proj.shape
    d = D_MODEL
    half = ATTN_HALF
    tp = ATTN_TP
    assert s % tp == 0 and tp % (dilation * half) == 0 and ATTN_PERM % dilation == 0
    nt = s // tp
    w = tp // dilation
    g = 1 if w > half else ATTN_RES_PER_STEP
    nres = dilation // g
    base = gi * 3
    hpt = tp // half

    def cur(which):
        return pl.BlockSpec((None, g * w, d), lambda bi, t, r: (bi, t * nres + r, base + which))

    if g == 1:
        wh = w // half

        def prev(which):
            return pl.BlockSpec((None, half, d),
                                lambda bi, t, r: (bi, jnp.maximum((t - 1) * hpt + (r + 1) * wh - 1, 0), base + which))

        def nxt(which):
            return pl.BlockSpec((None, half, d),
                                lambda bi, t, r: (bi, jnp.minimum(t + 1, nt - 1) * hpt + r * wh, base + which))
    else:
        def prev(which):
            return pl.BlockSpec((None, g * w, d), lambda bi, t, r: (bi, jnp.maximum(t - 1, 0) * nres + r, base + which))

        def nxt(which):
            return pl.BlockSpec((None, g * w, d),
                                lambda bi, t, r: (bi, jnp.minimum(t + 1, nt - 1) * nres + r, base + which))

    perm = ATTN_PERM
    if dilation == perm:
        nblk, lead = g, (b, nt, perm)
        omap = lambda bi, t, r: (bi, t, r, 0, 0)
        oblock = lambda last: (None, None, nblk, half, last)
    else:
        nblk, lead = perm // dilation, (b, nt, perm // dilation, dilation)
        omap = lambda bi, t, r: (bi, t, 0, r, 0, 0)
        oblock = lambda last: (None, None, nblk, None, half, last)

    o, lse = pl.pallas_call(
        functools.partial(_attn_kernel, dilation=dilation, w=w, g=g, sub_len=s // dilation),
        grid=(b, nt, nres),
        in_specs=[cur(0), prev(1), cur(1), nxt(1), prev(2), cur(2), nxt(2)],
        out_specs=[pl.BlockSpec(oblock(d), omap), pl.BlockSpec(oblock(LANES), omap)],
        out_shape=[jax.ShapeDtypeStruct(lead + (half, d), BF16), jax.ShapeDtypeStruct(lead + (half, LANES), F32)],
        scratch_shapes=[pltpu.VMEM((N_HEADS, g * w, HEAD_DIM), F32), pltpu.VMEM((g * w, LANES), F32)],
        compiler_params=_cparams("parallel", "parallel", "parallel"),
        name=f"dilated_attn_g{gi}",
    )(proj, proj, proj, proj, proj, proj, proj)
    return o.reshape(b * s, d), lse.reshape(b * s, LANES)


def _attn_out_kernel(o0_ref, o1_ref, o2_ref, l0_ref, l1_ref, l2_ref, wo_ref, x_ref, out_ref, y_scr):
    parts = []
    for h in range(N_HEADS):
        sl = slice(h * HEAD_DIM, (h + 1) * HEAD_DIM)
        a0, a1, a2 = l0_ref[:, h:h + 1], l1_ref[:, h:h + 1], l2_ref[:, h:h + 1]
        mx = jnp.maximum(jnp.maximum(a0, a1), a2)
        e0, e1, e2 = jnp.exp(a0 - mx), jnp.exp(a1 - mx), jnp.exp(a2 - mx)
        o = (e0 * o0_ref[:, sl].astype(F32) + e1 * o1_ref[:, sl].astype(F32) + e2 * o2_ref[:, sl].astype(F32))
        parts.append((o / (e0 + e1 + e2)).astype(BF16))
    y = _dot(jnp.concatenate(parts, axis=1), wo_ref[...])
    ncol = y_scr.shape[0]
    for c in range(ncol):
        y_scr[c] = y[:, c * LANES:(c + 1) * LANES]
    per = ATTN_TP // ATTN_PERM
    for l in range(per):
        rows = slice(l * ATTN_PERM, (l + 1) * ATTN_PERM)
        for c in range(ncol):
            cols = slice(c * LANES, (c + 1) * LANES)
            out_ref[rows, cols] = x_ref[rows, cols] + y_scr[c, pl.ds(l, ATTN_PERM, stride=per), :]


def attn_out(os, lses, w_out, x):
    n, d = x.shape
    tm = ATTN_TP
    tile = lambda i: (i, 0)
    big = pl.BlockSpec((tm, d), tile)
    small = pl.BlockSpec((tm, LANES), tile)
    return pl.pallas_call(
        _attn_out_kernel,
        grid=(n // tm,),
        in_specs=[big, big, big, small, small, small, pl.BlockSpec((d, d), lambda i: (0, 0)), big],
        out_specs=big,
        out_shape=jax.ShapeDtypeStruct((n, d), F32),
        scratch_shapes=[pltpu.VMEM((d // LANES, tm, LANES), F32)],
        compiler_params=_cparams("parallel"),
        name="attn_out",
    )(*os, *lses, w_out, x)


def _attn_proj_kernel(x_ref, g_ref, w_ref, o_ref, h_ref, acc_ref, *, blocks_per_group):
    j = pl.program_id(1)

    @pl.when(j == 0)
    def _():
        h_ref[...] = _rms(x_ref[...], g_ref[...]).astype(BF16)

    res = _dot(h_ref[...], w_ref[...])
    ncol = acc_ref.shape[0]
    for c in range(ncol):
        acc_ref[c] = res[:, c * LANES:(c + 1) * LANES]
    gi = j // blocks_per_group
    for gidx, (_, dilation) in enumerate(ATTN_GROUPS):
        @pl.when(gi == gidx)
        def _():
            if dilation == 1:
                o_ref[...] = res.astype(o_ref.dtype)
            else:
                w = ATTN_TP // dilation
                for r in range(dilation):
                    for c in range(ncol):
                        o_ref[r * w:(r + 1) * w, c * LANES:(c + 1) * LANES] = (
                            acc_ref[c, pl.ds(r, w, stride=dilation), :].astype(o_ref.dtype))


def attn_proj(x, g, w, tn=1024):
    n, d = x.shape
    nout = w.shape[1]
    tm = ATTN_TP
    group_cols = nout // len(ATTN_GROUPS)
    assert group_cols % tn == 0
    return pl.pallas_call(
        functools.partial(_attn_proj_kernel, blocks_per_group=group_cols // tn),
        grid=(n // tm, nout // tn),
        in_specs=[pl.BlockSpec((tm, d), lambda i, j: (i, 0)),
                  pl.BlockSpec((1, d), lambda i, j: (0, 0)),
                  pl.BlockSpec((d, tn), lambda i, j: (0, j))],
        out_specs=pl.BlockSpec((tm, tn), lambda i, j: (i, j)),
        out_shape=jax.ShapeDtypeStruct((n, nout), BF16),
        scratch_shapes=[pltpu.VMEM((tm, d), BF16), pltpu.VMEM((tn // LANES, tm, LANES), F32)],
        compiler_params=_cparams("parallel", "arbitrary"),
        name="attn_proj",
    )(x, g.reshape(1, d), w)


def _route_kernel(x_ref, g_ref, r_ref, h_ref, aff_ref):
    h = _rms(x_ref[...], g_ref[...])
    h_ref[...] = h.astype(BF16)
    logits = jnp.dot(h, r_ref[...], precision=lax.Precision.HIGHEST, preferred_element_type=F32)
    lt = logits.T[:N_EXPERTS, :]
    m = jnp.max(lt, axis=0, keepdims=True)
    e = jnp.exp(lt - m)
    aff_ref[...] = e / jnp.sum(e, axis=0, keepdims=True)


def moe_route(x, g, router, tm=512):
    n, d = x.shape
    rpad = jnp.zeros((d, LANES), F32).at[:, :N_EXPERTS].set(router)
    return pl.pallas_call(
        _route_kernel,
        grid=(n // tm,),
        in_specs=[pl.BlockSpec((tm, d), lambda i: (i, 0)),
                  pl.BlockSpec((1, d), lambda i: (0, 0)),
                  pl.BlockSpec((d, LANES), lambda i: (0, 0))],
        out_specs=[pl.BlockSpec((tm, d), lambda i: (i, 0)),
                   pl.BlockSpec((N_EXPERTS, tm), lambda i: (0, i))],
        out_shape=[jax.ShapeDtypeStruct((n, d), BF16), jax.ShapeDtypeStruct((N_EXPERTS, n), F32)],
        compiler_params=_cparams("parallel"),
        name="moe_route",
    )(x, g.reshape(1, d), rpad)


def _select_kernel(aff_ref, slot_ref, tab_ref, *, cap):
    e, n = aff_ref.shape
    bits = pltpu.bitcast(aff_ref[...], I32)
    tidx = lax.broadcasted_iota(I32, (e, n), 1)

    def count(mask_i32):
        return jnp.sum(mask_i32.astype(F32), axis=1, keepdims=True).astype(I32)

    def thr_body(it, thr):
        cand = thr | jnp.left_shift(jnp.int32(1), 30 - it)
        cnt = count(jnp.where(bits >= cand, 1, 0))
        return jnp.where(cnt >= cap, cand, thr)

    thr = lax.fori_loop(0, 31, thr_body, jnp.zeros((e, 1), I32))
    gt = jnp.where(bits > thr, 1, 0)
    eq = jnp.where(bits == thr, 1, 0)
    need = cap - count(gt)
    nbits = max(1, (n - 1).bit_length())

    def tie_body(it, j):
        cand = j | jnp.left_shift(jnp.int32(1), nbits - 1 - it)
        cnt = count(jnp.where(tidx < cand, eq, 0))
        return jnp.where(cnt < need, cand, j)

    jlast = lax.fori_loop(0, nbits, tie_body, jnp.zeros((e, 1), I32))
    sel = gt + jnp.where(tidx <= jlast, eq, 0)
    slot_ref[...] = sel
    rr = lax.broadcasted_iota(I32, (LANES, LANES), 0)
    cc = lax.broadcasted_iota(I32, (LANES, LANES), 1)
    upper = jnp.where(rr <= cc, 1.0, 0.0).astype(BF16)

    nblk = n // LANES
    blk_lane = lax.broadcasted_iota(I32, (e, nblk), 1)

    tab_ref[...] = jnp.zeros((e, nblk), I32)

    def blk_body(bi, run):
        off = pl.multiple_of(bi * LANES, LANES)
        sb = slot_ref[:, pl.ds(off, LANES)]
        inc = _dot(sb.astype(F32).astype(BF16), upper).astype(I32)
        slot_ref[:, pl.ds(off, LANES)] = jnp.where(sb > 0, inc - sb + run, -1)
        tab_ref[...] = jnp.where(blk_lane == bi, run, tab_ref[...])
        return run + inc[:, LANES - 1:LANES]

    lax.fori_loop(0, nblk, blk_body, jnp.zeros((e, 1), I32))


def moe_select(aff_t, cap):
    e, n = aff_t.shape
    return pl.pallas_call(
        functools.partial(_select_kernel, cap=cap),
        grid=(1,),
        in_specs=[pl.BlockSpec((e, n), lambda i: (0, 0))],
        out_specs=[pl.BlockSpec((e, n), lambda i: (0, 0)), pl.BlockSpec((e, n // LANES), lambda i: (0, 0))],
        out_shape=[jax.ShapeDtypeStruct((e, n), I32), jax.ShapeDtypeStruct((e, n // LANES), I32)],
        compiler_params=_cparams("arbitrary"),
        name="moe_select",
    )(aff_t)


def _ffn_kernel(x_ref, wg_ref, wu_ref, wd_ref, o_ref):
    x = x_ref[...]
    acc = None
    for f in range(EXPERT_FF // FF_CHUNK):
        fs = slice(f * FF_CHUNK, (f + 1) * FF_CHUNK)
        hid = (_silu(_dot(x, wg_ref[:, fs])) * _dot(x, wu_ref[:, fs])).astype(BF16)
        part = _dot(hid, wd_ref[fs, :])
        acc = part if acc is None else acc + part
    o_ref[...] = acc.astype(o_ref.dtype)


def moe_ffn(xe, w_gate, w_up, w_down, layer, tm=1024):
    e, c, d = xe.shape
    f = w_gate.shape[3]
    return pl.pallas_call(
        _ffn_kernel,
        grid=(e, c // tm),
        in_specs=[pl.BlockSpec((None, tm, d), lambda ei, j: (ei, j, 0)),
                  pl.BlockSpec((None, None, d, f), lambda ei, j: (layer, ei, 0, 0)),
                  pl.BlockSpec((None, None, d, f), lambda ei, j: (layer, ei, 0, 0)),
                  pl.BlockSpec((None, None, f, d), lambda ei, j: (layer, ei, 0, 0))],
        out_specs=pl.BlockSpec((None, tm, d), lambda ei, j: (ei, j, 0)),
        out_shape=jax.ShapeDtypeStruct((e, c, d), BF16),
        compiler_params=_cparams("parallel", "arbitrary"),
        name="moe_ffn",
    )(xe, w_gate, w_up, w_down)


def _slot_ranges(tab_ref, i, cap):
    out = []
    for e in range(N_EXPERTS):
        p0 = tab_ref[i * N_EXPERTS + e]
        out.append((e * cap + p0, tab_ref[(i + 1) * N_EXPERTS + e] - p0))
    return out


def _dispatch_kernel(tab_ref, h_ref, slot_ref, xe_ref, stage_ref, tail_ref, sem, npiece_ref, *, cap, tile):
    i = pl.program_id(0)
    last = pl.num_programs(0) - 1
    slot = i % 2
    grp = BF16_SUBLANES
    kc = COMBINE_KCHUNK

    def drain(sl):
        def body(_, carry):
            pltpu.make_async_copy(stage_ref.at[sl, pl.ds(0, grp)], xe_ref.at[pl.ds(0, grp)], sem.at[sl]).wait()
            return carry
        lax.fori_loop(0, npiece_ref[sl], body, 0)

    @pl.when(i >= 2)
    def _():
        drain(slot)

    meta, bases = [], []
    off = jnp.int32(0)
    for g0, cnt in _slot_ranges(tab_ref, i, cap):
        a = (g0 // grp) * grp
        lead = g0 - a
        tot = lead + cnt
        meta.append((off, a, lead, tot, (tot // grp) * grp))
        bases.append(off - a)
        off = off + ((tot + grp - 1) // grp) * grp
    row_e = lax.broadcasted_iota(I32, (N_EXPERTS, 1), 0)
    cvec = jnp.zeros((N_EXPERTS, 1), I32)
    for e in range(N_EXPERTS):
        cvec = jnp.where(row_e == e, bases[e] + e * cap, cvec)
    slot_e = slot_ref[...]
    col = jnp.where(slot_e >= 0, slot_e + cvec, -1)

    def chunk_body(ci, carry):
        base = pl.multiple_of(ci * kc, kc)
        rows = lax.broadcasted_iota(I32, (kc, tile), 0) + base
        p = jnp.zeros((kc, tile), F32)
        for e in range(N_EXPERTS):
            p = jnp.where(col[e:e + 1, :] == rows, 1.0, p)
        stage_ref[slot, pl.ds(base, kc), :] = _dot(p.astype(BF16), h_ref[...]).astype(stage_ref.dtype)
        return carry

    lax.fori_loop(0, (off + kc - 1) // kc, chunk_body, 0)

    row_g = lax.broadcasted_iota(I32, (grp, D_MODEL), 0)
    npiece = jnp.int32(0)
    for e, (off_e, a, lead, tot, nfull) in enumerate(meta):
        first = pl.ds(pl.multiple_of(off_e, grp), grp)

        @pl.when(lead > 0)
        def _():
            stage_ref[slot, first, :] = jnp.where(row_g < lead, tail_ref[e], stage_ref[slot, first, :])

        @pl.when(tot > nfull)
        def _():
            tail_ref[e] = stage_ref[slot, pl.ds(pl.multiple_of(off_e + nfull, grp), grp), :]

        def piece(k, carry):
            pltpu.make_async_copy(stage_ref.at[slot, pl.ds(pl.multiple_of(off_e + k * grp, grp), grp)],
                                  xe_ref.at[pl.ds(pl.multiple_of(a + k * grp, grp), grp)], sem.at[slot]).start()
            return carry

        lax.fori_loop(0, nfull // grp, piece, 0)
        npiece = npiece + nfull // grp
    npiece_ref[slot] = npiece

    @pl.when(i == last)
    def _():
        drain(slot)

        @pl.when(i >= 1)
        def _():
            drain(1 - slot)


def moe_dispatch(hb, slot, tab, cap, tile=COMBINE_TILE):
    n, d = hb.shape
    kmax = N_EXPERTS * (tile + BF16_SUBLANES)
    kmax = -(-kmax // COMBINE_KCHUNK) * COMBINE_KCHUNK
    grid_spec = pltpu.PrefetchScalarGridSpec(
        num_scalar_prefetch=1,
        grid=(n // tile,),
        in_specs=[pl.BlockSpec((tile, d), lambda i, tab: (i, 0)),
                  pl.BlockSpec((N_EXPERTS, tile), lambda i, tab: (0, i))],
        out_specs=pl.BlockSpec(memory_space=pl.ANY),
        scratch_shapes=[pltpu.VMEM((2, kmax, d), BF16), pltpu.VMEM((N_EXPERTS, BF16_SUBLANES, d), BF16),
                        pltpu.SemaphoreType.DMA((2,)), pltpu.SMEM((2,), I32)],
    )
    return pl.pallas_call(
        functools.partial(_dispatch_kernel, cap=cap, tile=tile),
        grid_spec=grid_spec,
        out_shape=jax.ShapeDtypeStruct((N_EXPERTS * cap, d), BF16),
        compiler_params=_cparams("arbitrary"),
        name="moe_dispatch",
    )(tab, hb, slot)


def _combine_ranges(tab_ref, i, cap):
    grp = BF16_SUBLANES
    slabs, bases = [], []
    off = jnp.int32(0)
    for e, (g0, cnt) in enumerate(_slot_ranges(tab_ref, i, cap)):
        a = (g0 // grp) * grp
        nrow = jnp.where(cnt > 0, ((g0 - a + cnt + grp - 1) // grp) * grp, 0)
        slabs.append((off, a, nrow))
        bases.append(off - a + e * cap)
        off = off + nrow
    return slabs, bases, off


def _combine_copies(tab_ref, ye_ref, zbuf_ref, sem, i, cap, tile):
    grp = BF16_SUBLANES
    max_shift = (tile // grp + 1).bit_length() - 1
    copies = []
    for off_e, a, nrow in _combine_ranges(tab_ref, i, cap)[0]:
        src, dst = a, off_e
        for sh in range(max_shift, -1, -1):
            size = grp << sh
            take = (nrow & size) != 0
            copies.append((take, pltpu.make_async_copy(ye_ref.at[pl.ds(pl.multiple_of(src, grp), size)],
                                                       zbuf_ref.at[pl.ds(pl.multiple_of(dst, grp), size)], sem)))
            step = jnp.where(take, size, 0)
            src, dst = src + step, dst + step
    return copies


def _combine_kernel(tab_ref, x_ref, slot_ref, aff_ref, ye_ref, g_ref, o_ref, zbuf_ref, sem,
                    *, cap, tile, final_norm):
    i = pl.program_id(0)
    slot = i % 2

    def start_tile(ti, sl):
        for take, cp in _combine_copies(tab_ref, ye_ref, zbuf_ref.at[sl], sem.at[sl], ti, cap, tile):
            @pl.when(take)
            def _():
                cp.start()

    @pl.when(i == 0)
    def _():
        zbuf_ref[...] = jnp.zeros_like(zbuf_ref)
        start_tile(0, 0)

    @pl.when(i + 1 < pl.num_programs(0))
    def _():
        start_tile(i + 1, 1 - slot)

    zcur_ref = zbuf_ref.at[slot]
    _, bases, k_tot = _combine_ranges(tab_ref, i, cap)
    lane_e = lax.broadcasted_iota(I32, (1, N_EXPERTS), 1)
    cvec = jnp.zeros((1, N_EXPERTS), I32)
    for e in range(N_EXPERTS):
        cvec = jnp.where(lane_e == e, bases[e], cvec)
    slot_t = slot_ref[...]
    col = jnp.where(slot_t >= 0, slot_t + cvec, -1)
    aff = aff_ref[...]
    o_ref[...] = x_ref[...]

    for take, cp in _combine_copies(tab_ref, ye_ref, zcur_ref, sem.at[slot], i, cap, tile):
        @pl.when(take)
        def _():
            cp.wait()

    kc = COMBINE_KCHUNK

    def chunk_body(ci, carry):
        base = pl.multiple_of(ci * kc, kc)
        z = zcur_ref[pl.ds(base, kc), :]
        lanes = lax.broadcasted_iota(I32, (tile, kc), 1) + base
        p = jnp.zeros((tile, kc), F32)
        for e in range(N_EXPERTS):
            p = jnp.where(col[:, e:e + 1] == lanes, aff[:, e:e + 1], p)
        p_hi = p.astype(BF16)
        p_lo = (p - p_hi.astype(F32)).astype(BF16)
        o_ref[...] += _dot(p_hi, z) + _dot(p_lo, z)
        return carry

    lax.fori_loop(0, (k_tot + kc - 1) // kc, chunk_body, 0)
    if final_norm:
        o_ref[...] = _rms(o_ref[...], g_ref[...])


def moe_combine(x, slot_t, aff, tab, ye, cap, norm_g=None, tile=COMBINE_TILE):
    n, d = x.shape
    kmax = N_EXPERTS * (tile + BF16_SUBLANES)
    kmax = -(-kmax // COMBINE_KCHUNK) * COMBINE_KCHUNK
    final_norm = norm_g is not None
    g = (norm_g if final_norm else jnp.ones((d,), F32)).reshape(1, d)
    grid_spec = pltpu.PrefetchScalarGridSpec(
        num_scalar_prefetch=1,
        grid=(n // tile,),
        in_specs=[pl.BlockSpec((tile, d), lambda i, tab: (i, 0)),
                  pl.BlockSpec((tile, N_EXPERTS), lambda i, tab: (i, 0)),
                  pl.BlockSpec((tile, N_EXPERTS), lambda i, tab: (i, 0)),
                  pl.BlockSpec(memory_space=pl.ANY),
                  pl.BlockSpec((1, d), lambda i, tab: (0, 0))],
        out_specs=pl.BlockSpec((tile, d), lambda i, tab: (i, 0)),
        scratch_shapes=[pltpu.VMEM((2, kmax, d), BF16), pltpu.SemaphoreType.DMA((2,))],
    )
    return pl.pallas_call(
        functools.partial(_combine_kernel, cap=cap, tile=tile, final_norm=final_norm),
        grid_spec=grid_spec,
        out_shape=jax.ShapeDtypeStruct((n, d), F32),
        compiler_params=_cparams("arbitrary"),
        name="moe_combine",
    )(tab, x, slot_t, aff, ye, g)


def expert_choice_moe(x, norm_g, router, w_gate, w_up, w_down, layer, final_g=None):
    n, d = x.shape
    cap = max(1, CAPACITY_FACTOR * n // N_EXPERTS)
    hb, aff_t = moe_route(x, norm_g, router)
    slot, blk_tab = moe_select(aff_t, cap)
    tab = jnp.concatenate([blk_tab[:, ::COMBINE_TILE // LANES].T, jnp.full((1, N_EXPERTS), cap, I32)], axis=0)
    tab = tab.reshape(-1)
    xe = moe_dispatch(hb, slot, tab, cap).reshape(N_EXPERTS, cap, d)
    ye = moe_ffn(xe, w_gate, w_up, w_down, layer).reshape(N_EXPERTS * cap, d)
    return moe_combine(x, slot.T, aff_t.T, tab, ye, cap, norm_g=final_g)


def _trunk(x, p):
    b, s, d = x.shape
    n = b * s
    x = x.reshape(n, d)
    for i in range(DEPTH):
        kind, j = i % N_MIXERS, i // N_MIXERS
        if kind == 0:
            u = norm_glu(x, p["norm_mix"][i], p["conv_w_glu"][j], p["conv_b_glu"][j])
            x = conv_out(u.reshape(b, s, d), x.reshape(b, s, d), p["conv_w_dw"][j], p["conv_b_dw"][j],
                         p["conv_ln_g"][j], p["conv_ln_b"][j], p["conv_w_out"][j], p["conv_b_out"][j]).reshape(n, d)
        elif kind == 1:
            proj = norm_linear(x, p["norm_mix"][i], p["dn_w_in"][j], F32, tn=768)
            q, k, v, bg = dn_prep(proj.reshape(b, s, -1), p["dn_w_conv"][j], p["dn_a_log"][j], p["dn_dt_bias"][j])
            o_f, o_b = delta_rule(q, k, v, bg)
            x = dn_out(o_f.reshape(n, d), o_b.reshape(n, d), proj, p["dn_norm_g"][j], p["dn_w_out"][j], x)
        else:
            proj = attn_proj(x, p["norm_mix"][i], p["attn_w_in"][j]).reshape(b, s, -1)
            os, lses = [], []
            for gi, (window, dilation) in enumerate(ATTN_GROUPS):
                assert window // (2 * dilation) == ATTN_HALF
                o, lse = dilated_attention(proj, gi, dilation)
                os.append(o)
                lses.append(lse)
            x = attn_out(os, lses, p["attn_w_out"][j], x)
        x = expert_choice_moe(x, p["norm_ffn"][i], p["moe_router"][i], p["moe_w_gate"], p["moe_w_up"],
                              p["moe_w_down"], i, final_g=p["norm_final"] if i == DEPTH - 1 else None)
    return x.reshape(b, s, d)


def kernel(x_prompt, x_sample, norm_mix, norm_ffn, norm_final, conv_w_glu, conv_b_glu, conv_w_dw, conv_b_dw,
           conv_ln_g, conv_ln_b, conv_w_out, conv_b_out, dn_w_in, dn_w_conv, dn_a_log, dn_dt_bias, dn_norm_g,
           dn_w_out, attn_w_in, attn_w_out, moe_router, moe_w_gate, moe_w_up, moe_w_down):
    dn_pad = (-dn_w_in.shape[-1]) % 512
    p = dict(
        norm_mix=norm_mix, norm_ffn=norm_ffn, norm_final=norm_final,
        conv_w_glu=conv_w_glu.astype(BF16), conv_b_glu=conv_b_glu, conv_w_dw=conv_w_dw, conv_b_dw=conv_b_dw,
        conv_ln_g=conv_ln_g, conv_ln_b=conv_ln_b, conv_w_out=conv_w_out.astype(BF16), conv_b_out=conv_b_out,
        dn_w_in=jnp.pad(dn_w_in, ((0, 0), (0, 0), (0, dn_pad))).astype(BF16), dn_w_conv=dn_w_conv,
        dn_a_log=dn_a_log, dn_dt_bias=dn_dt_bias, dn_norm_g=dn_norm_g, dn_w_out=dn_w_out.astype(BF16),
        attn_w_in=attn_w_in.astype(BF16), attn_w_out=attn_w_out.astype(BF16),
        moe_router=moe_router, moe_w_gate=moe_w_gate.astype(BF16), moe_w_up=moe_w_up.astype(BF16),
        moe_w_down=moe_w_down.astype(BF16),
    )
    return (_trunk(x_prompt, p), _trunk(x_sample, p))
```

```python
import functools
import math

import jax
import jax.numpy as jnp
from jax import lax
from jax.experimental import pallas as pl
from jax.experimental.pallas import tpu as pltpu

F32 = jnp.float32
BF16 = jnp.bfloat16
I32 = jnp.int32

D_MODEL = 1024
DEPTH = 4
N_MIXERS = 3
RMS_EPS = 1e-6
LN_EPS = 1e-5
NEG_INF = -1e30
CONV_WIDTH = 31
HEAD_DIM = 128
N_HEADS = D_MODEL // HEAD_DIM
SHORT_CONV = 5
ATTN_GROUPS = ((128, 1), (512, 4), (2048, 16))
N_EXPERTS = 16
EXPERT_FF = 2 * D_MODEL
CAPACITY_FACTOR = 2

VMEM_LIMIT_BYTES = 56 * 1024 * 1024
LANES = 128
BF16_SUBLANES = 16

DN_CHUNK = 64
DN_SPLIT_PASSES = False
ATTN_HALF = 64
ATTN_TQ = 128
ATTN_PERM = 16
ATTN_TP = ATTN_PERM * ATTN_HALF
ATTN_RES_PER_STEP = 4
FF_CHUNK = 512
COMBINE_TILE = 256
COMBINE_KCHUNK = 256


def _cparams(*sem):
    return pltpu.CompilerParams(dimension_semantics=sem, vmem_limit_bytes=VMEM_LIMIT_BYTES)


def _rms(x, g):
    return x * lax.rsqrt(jnp.mean(x * x, axis=-1, keepdims=True) + RMS_EPS) * g


def _silu(x):
    return x * jax.nn.sigmoid(x)


def _dot(a, b):
    return jnp.dot(a, b, preferred_element_type=F32)


def _dot_nt(a, b):
    return lax.dot_general(a, b, (((1,), (1,)), ((), ())), preferred_element_type=F32)


def _dot_tn(a, b):
    return lax.dot_general(a, b, (((0,), (0,)), ((), ())), preferred_element_type=F32)


def _norm_linear_kernel(x_ref, g_ref, w_ref, o_ref, h_ref):
    @pl.when(pl.program_id(1) == 0)
    def _():
        h_ref[...] = _rms(x_ref[...], g_ref[...]).astype(BF16)

    o_ref[...] = _dot(h_ref[...], w_ref[...]).astype(o_ref.dtype)


def norm_linear(x, g, w, out_dtype, tm=1024, tn=512):
    n, d = x.shape
    nout = w.shape[1]
    return pl.pallas_call(
        _norm_linear_kernel,
        grid=(n // tm, nout // tn),
        in_specs=[pl.BlockSpec((tm, d), lambda i, j: (i, 0)),
                  pl.BlockSpec((1, d), lambda i, j: (0, 0)),
                  pl.BlockSpec((d, tn), lambda i, j: (0, j))],
        out_specs=pl.BlockSpec((tm, tn), lambda i, j: (i, j)),
        out_shape=jax.ShapeDtypeStruct((n, nout), out_dtype),
        scratch_shapes=[pltpu.VMEM((tm, d), BF16)],
        compiler_params=_cparams("parallel", "arbitrary"),
        name="norm_linear",
    )(x, g.reshape(1, d), w)


def _norm_glu_kernel(x_ref, g_ref, wa_ref, wg_ref, ba_ref, bg_ref, o_ref, h_ref):
    @pl.when(pl.program_id(1) == 0)
    def _():
        h_ref[...] = _rms(x_ref[...], g_ref[...]).astype(BF16)

    h = h_ref[...]
    a = _dot(h, wa_ref[...]) + ba_ref[...]
    gate = _dot(h, wg_ref[...]) + bg_ref[...]
    o_ref[...] = a * jax.nn.sigmoid(gate)


def norm_glu(x, g, w, b, tm=1024, tn=1024):
    n, d = x.shape
    nb = d // tn
    b2 = b.reshape(1, 2 * d)
    return pl.pallas_call(
        _norm_glu_kernel,
        grid=(n // tm, nb),
        in_specs=[pl.BlockSpec((tm, d), lambda i, j: (i, 0)),
                  pl.BlockSpec((1, d), lambda i, j: (0, 0)),
                  pl.BlockSpec((d, tn), lambda i, j: (0, j)),
                  pl.BlockSpec((d, tn), lambda i, j: (0, j + nb)),
                  pl.BlockSpec((1, tn), lambda i, j: (0, j)),
                  pl.BlockSpec((1, tn), lambda i, j: (0, j + nb))],
        out_specs=pl.BlockSpec((tm, tn), lambda i, j: (i, j)),
        out_shape=jax.ShapeDtypeStruct((n, d), F32),
        scratch_shapes=[pltpu.VMEM((tm, d), BF16)],
        compiler_params=_cparams("parallel", "arbitrary"),
        name="norm_glu",
    )(x, g.reshape(1, d), w, w, b2, b2)


CONV_HALO = 16


def _conv_out_kernel(up_ref, uc_ref, un_ref, wdw_ref, bdw_ref, lng_ref, lnb_ref, wo_ref, bo_ref, x_ref,
                     o_ref, win_ref, shift_ref, *, ts):
    i = pl.program_id(1)
    last = pl.num_programs(1) - 1
    win_ref[0:CONV_HALO, :] = jnp.where(i > 0, up_ref[...], 0.0)
    win_ref[CONV_HALO:CONV_HALO + ts, :] = uc_ref[...]
    win_ref[CONV_HALO + ts:, :] = jnp.where(i < last, un_ref[...], 0.0)
    for s in range(1, 8):
        shift_ref[s - 1] = win_ref[pl.ds(s, ts + 24), :]
    acc = jnp.zeros((ts, D_MODEL), F32) + bdw_ref[...]
    for s in range(8):
        for a in range(4):
            j = 8 * a + s - 1
            if 0 <= j < CONV_WIDTH:
                rows = slice(8 * a, 8 * a + ts)
                tap = win_ref[rows, :] if s == 0 else shift_ref[s - 1, rows, :]
                acc = acc + wdw_ref[j:j + 1, :] * tap
    mu = jnp.mean(acc, axis=-1, keepdims=True)
    cen = acc - mu
    var = jnp.mean(cen * cen, axis=-1, keepdims=True)
    u = cen * lax.rsqrt(var + LN_EPS) * lng_ref[...] + lnb_ref[...]
    y = _silu(u).astype(BF16)
    o_ref[...] = x_ref[...] + _dot(y, wo_ref[...]) + bo_ref[...]


def conv_out(u, x, w_dw, b_dw, ln_g, ln_b, w_out, b_out, ts=256):
    b, s, d = u.shape
    hb = ts // CONV_HALO
    nblk = s // CONV_HALO
    wdw = jnp.concatenate([w_dw, jnp.zeros((1, d), F32)], axis=0)
    row = lambda v: v.reshape(1, d)
    return pl.pallas_call(
        functools.partial(_conv_out_kernel, ts=ts),
        grid=(b, s // ts),
        in_specs=[pl.BlockSpec((None, CONV_HALO, d), lambda bi, i: (bi, jnp.maximum(i * hb - 1, 0), 0)),
                  pl.BlockSpec((None, ts, d), lambda bi, i: (bi, i, 0)),
                  pl.BlockSpec((None, CONV_HALO, d), lambda bi, i: (bi, jnp.minimum((i + 1) * hb, nblk - 1), 0)),
                  pl.BlockSpec((CONV_WIDTH + 1, d), lambda bi, i: (0, 0)),
                  pl.BlockSpec((1, d), lambda bi, i: (0, 0)),
                  pl.BlockSpec((1, d), lambda bi, i: (0, 0)),
                  pl.BlockSpec((1, d), lambda bi, i: (0, 0)),
                  pl.BlockSpec((d, d), lambda bi, i: (0, 0)),
                  pl.BlockSpec((1, d), lambda bi, i: (0, 0)),
                  pl.BlockSpec((None, ts, d), lambda bi, i: (bi, i, 0))],
        out_specs=pl.BlockSpec((None, ts, d), lambda bi, i: (bi, i, 0)),
        out_shape=jax.ShapeDtypeStruct((b, s, d), F32),
        scratch_shapes=[pltpu.VMEM((ts + 2 * CONV_HALO, d), F32), pltpu.VMEM((7, ts + 24, d), F32)],
        compiler_params=_cparams("parallel", "parallel"),
        name="conv_out",
    )(u, u, u, wdw, row(b_dw), row(ln_g), row(ln_b), w_out, row(b_out), x)


DN_HALO = 8
DN_BG = 32


def _dn_prep_kernel(pp_ref, pc_ref, pn_ref, ba_ref, wc_ref, alog_ref, dtb_ref,
                    q_ref, k_ref, v_ref, bg_ref, win_ref, *, ts):
    i = pl.program_id(1)
    last = pl.num_programs(1) - 1
    w3 = 3 * D_MODEL
    win_ref[0:DN_HALO, :] = jnp.where(i > 0, pp_ref[...], 0.0)
    win_ref[DN_HALO:DN_HALO + ts, :] = pc_ref[...]
    win_ref[DN_HALO + ts:, :] = jnp.where(i < last, pn_ref[...], 0.0)
    left = (SHORT_CONV - 1) // 2
    acc = jnp.zeros((ts, w3), F32)
    for j in range(SHORT_CONV):
        acc = acc + wc_ref[j:j + 1, :] * win_ref[pl.ds(DN_HALO - left + j, ts), :]
    qkv = _silu(acc)

    def l2n(t):
        return t * lax.rsqrt(jnp.sum(t * t, axis=-1, keepdims=True) + RMS_EPS)

    for h in range(N_HEADS):
        sl = slice(h * HEAD_DIM, (h + 1) * HEAD_DIM)
        q_ref[:, sl] = l2n(qkv[:, sl]) * (HEAD_DIM ** -0.5)
        k_ref[:, sl] = l2n(qkv[:, D_MODEL + h * HEAD_DIM:D_MODEL + (h + 1) * HEAD_DIM])
    v_ref[...] = qkv[:, 2 * D_MODEL:]
    ba = ba_ref[:, :DN_BG]
    lane = lax.broadcasted_iota(I32, (ts, DN_BG), 1)
    beta = jax.nn.sigmoid(ba)
    g = -jnp.exp(alog_ref[...]) * jax.nn.softplus(ba + dtb_ref[...])
    bg_ref[...] = jnp.where(lane < DN_BG // 2, beta, g)


def dn_prep(proj, w_conv, a_log, dt_bias, ts=256):
    b, s, _ = proj.shape
    d = D_MODEL
    hb = ts // DN_HALO
    nblk = s // DN_HALO
    zeros16 = jnp.zeros((16,), F32)
    alog = jnp.concatenate([zeros16, a_log.reshape(-1)]).reshape(1, DN_BG)
    dtb = jnp.concatenate([zeros16, dt_bias.reshape(-1)]).reshape(1, DN_BG)
    bacol = 4 * d // LANES
    tile = lambda bi, i: (bi, i, 0)
    return pl.pallas_call(
        functools.partial(_dn_prep_kernel, ts=ts),
        grid=(b, s // ts),
        in_specs=[pl.BlockSpec((None, DN_HALO, 3 * d), lambda bi, i: (bi, jnp.maximum(i * hb - 1, 0), 0)),
                  pl.BlockSpec((None, ts, 3 * d), tile),
                  pl.BlockSpec((None, DN_HALO, 3 * d), lambda bi, i: (bi, jnp.minimum((i + 1) * hb, nblk - 1), 0)),
                  pl.BlockSpec((None, ts, LANES), lambda bi, i: (bi, i, bacol)),
                  pl.BlockSpec((SHORT_CONV, 3 * d), lambda bi, i: (0, 0)),
                  pl.BlockSpec((1, DN_BG), lambda bi, i: (0, 0)),
                  pl.BlockSpec((1, DN_BG), lambda bi, i: (0, 0))],
        out_specs=[pl.BlockSpec((None, ts, d), tile),
                   pl.BlockSpec((None, ts, d), tile),
                   pl.BlockSpec((None, ts, d), tile),
                   pl.BlockSpec((None, ts, DN_BG), tile)],
        out_shape=[jax.ShapeDtypeStruct((b, s, d), F32)] * 3 + [jax.ShapeDtypeStruct((b, s, DN_BG), F32)],
        scratch_shapes=[pltpu.VMEM((ts + 2 * DN_HALO, 3 * d), F32)],
        compiler_params=_cparams("parallel", "parallel"),
        name="dn_prep",
    )(proj, proj, proj, proj, w_conv, alog, dtb)


def _delta_kernel(qf_ref, kf_ref, vf_ref, bgf_ref, qb_ref, kb_ref, vb_ref, bgb_ref,
                  of_ref, ob_ref, state_ref, *, chunk):
    @pl.when(pl.program_id(1) == 0)
    def _():
        state_ref[...] = jnp.zeros_like(state_ref)

    c = chunk
    dh = HEAD_DIM
    ns = 2 * dh
    r = lax.broadcasted_iota(I32, (c, c), 0)
    s = lax.broadcasted_iota(I32, (c, c), 1)
    eye = jnp.where(lax.broadcasted_iota(I32, (DN_BG, DN_BG), 0) == lax.broadcasted_iota(I32, (DN_BG, DN_BG), 1),
                    1.0, 0.0)
    hi = lax.Precision.HIGHEST
    chains = []
    for di, (q_ref, k_ref, v_ref, bg_ref, o_ref) in enumerate(
            ((qf_ref, kf_ref, vf_ref, bgf_ref, of_ref), (qb_ref, kb_ref, vb_ref, bgb_ref, ob_ref))):
        reverse = di == 1
        incl = (r <= s) if reverse else (r >= s)
        strict = (r < s) if reverse else (r > s)
        bg = bg_ref[...]
        gc_cols = jnp.dot(jnp.where(incl, 1.0, 0.0), bg, precision=hi, preferred_element_type=F32)
        gc_rows = lax.dot_general(eye, gc_cols, (((1,), (1,)), ((), ())),
                                  precision=hi, preferred_element_type=F32)
        for h in range(N_HEADS):
            jb = di * N_HEADS + h
            jg = DN_BG // 2 + jb
            gc_col = gc_cols[:, jg:jg + 1]
            chains.append(dict(
                sl=slice(h * dh, (h + 1) * dh), jb=jb, q_ref=q_ref, k_ref=k_ref, v_ref=v_ref, o_ref=o_ref,
                incl=incl, strict=strict, beta=bg[:, jb:jb + 1], gc_col=gc_col, gc_row=gc_rows[jg:jg + 1, :],
                g_tot=gc_col[0:1, :] if reverse else gc_col[c - 1:c, :]))

    for ch in chains:
        q, k, v = ch["q_ref"][:, ch["sl"]], ch["k_ref"][:, ch["sl"]], ch["v_ref"][:, ch["sl"]]
        decay = jnp.exp(jnp.where(ch["incl"], ch["gc_col"] - ch["gc_row"], NEG_INF))
        kb = k * ch["beta"]
        egc = jnp.exp(ch["gc_col"])
        qk_kk = _dot_nt(jnp.concatenate([q, kb], axis=0).astype(BF16), k.astype(BF16))
        ch["a"] = jnp.where(ch["incl"], qk_kk[:c] * decay, 0.0).astype(BF16)
        m = jnp.where(ch["strict"], -qk_kk[c:] * decay, 0.0)
        ch["x"] = jnp.concatenate([v * ch["beta"], kb * egc, m], axis=1)
        ch["qd"] = (q * egc).astype(BF16)
        ch["kd"] = (k * jnp.exp(ch["g_tot"] - ch["gc_col"])).astype(BF16)

    n_levels = max(1, (c - 1).bit_length())
    for lvl in range(n_levels):
        width = ns + c if lvl + 1 < n_levels else ns
        for ch in chains:
            x = ch["x"]
            xh = x.astype(BF16)
            if DN_SPLIT_PASSES:
                xl = (x - xh.astype(F32)).astype(BF16)
                lhs = jnp.concatenate([xh[:, ns:], xh[:, ns:], xl[:, ns:]], axis=1)
                rhs = jnp.concatenate([xh[:, :width], xl[:, :width], xh[:, :width]], axis=0)
            else:
                lhs, rhs = xh[:, ns:], xh[:, :width]
            prod = _dot(lhs, rhs)
            sol = x[:, :ns] + prod[:, :ns]
            ch["x"] = jnp.concatenate([sol, prod[:, ns:]], axis=1) if width > ns else sol

    for ch in chains:
        ch["state"] = state_ref[ch["jb"]]
        lhs = jnp.concatenate([ch["x"][:, dh:].astype(BF16), ch["qd"]], axis=0)
        ch["ws"] = _dot(lhs, ch["state"].astype(BF16))
    for ch in chains:
        ch["vb"] = (ch["x"][:, :dh] - ch["ws"][:c]).astype(BF16)
        ch["o_ref"][:, ch["sl"]] = ch["ws"][c:] + _dot(ch["a"], ch["vb"])
    for ch in chains:
        state_ref[ch["jb"]] = ch["state"] * jnp.exp(ch["g_tot"]) + _dot_tn(ch["kd"], ch["vb"])


def delta_rule(q, k, v, bg, chunk=DN_CHUNK):
    b, s, d = q.shape
    nc = s // chunk
    fwd = lambda bi, c: (bi, c, 0)
    bwd = lambda bi, c: (bi, nc - 1 - c, 0)
    big = lambda im: pl.BlockSpec((None, chunk, d), im)
    return pl.pallas_call(
        functools.partial(_delta_kernel, chunk=chunk),
        grid=(b, nc),
        in_specs=[big(fwd), big(fwd), big(fwd), pl.BlockSpec((None, chunk, DN_BG), fwd),
                  big(bwd), big(bwd), big(bwd), pl.BlockSpec((None, chunk, DN_BG), bwd)],
        out_specs=[big(fwd), big(bwd)],
        out_shape=[jax.ShapeDtypeStruct((b, s, d), F32)] * 2,
        scratch_shapes=[pltpu.VMEM((2 * N_HEADS, HEAD_DIM, HEAD_DIM), F32)],
        compiler_params=_cparams("parallel", "arbitrary"),
        name="delta_rule",
    )(q, k, v, bg, q, k, v, bg)


def _dn_out_kernel(of_ref, ob_ref, z_ref, ng_ref, wo_ref, x_ref, o_ref):
    parts = []
    for h in range(N_HEADS):
        sl = slice(h * HEAD_DIM, (h + 1) * HEAD_DIM)
        o = of_ref[:, sl] + ob_ref[:, sl]
        o = o * lax.rsqrt(jnp.mean(o * o, axis=-1, keepdims=True) + RMS_EPS) * ng_ref[...] * _silu(z_ref[:, sl])
        parts.append(o.astype(BF16))
    o_ref[...] = x_ref[...] + _dot(jnp.concatenate(parts, axis=1), wo_ref[...])


def dn_out(o_f, o_b, proj, norm_g, w_out, x, tm=512):
    n, d = x.shape
    tile = lambda i: (i, 0)
    return pl.pallas_call(
        _dn_out_kernel,
        grid=(n // tm,),
        in_specs=[pl.BlockSpec((tm, d), tile), pl.BlockSpec((tm, d), tile),
                  pl.BlockSpec((tm, d), lambda i: (i, 3)),
                  pl.BlockSpec((1, HEAD_DIM), lambda i: (0, 0)),
                  pl.BlockSpec((d, d), lambda i: (0, 0)),
                  pl.BlockSpec((tm, d), tile)],
        out_specs=pl.BlockSpec((tm, d), tile),
        out_shape=jax.ShapeDtypeStruct((n, d), F32),
        compiler_params=_cparams("parallel"),
        name="dn_out",
    )(o_f, o_b, proj, norm_g.reshape(1, HEAD_DIM), w_out, x)


def _attn_kernel(q_ref, kp_ref, kc_ref, kn_ref, vp_ref, vc_ref, vn_ref, o_ref, lse_ref, o_scr, l_scr,
                 *, dilation, w, g, sub_len):
    t = pl.program_id(1)
    half = ATTN_HALF
    tq = min(ATTN_TQ, w)
    nsub = w // tq
    tk = tq + 2 * half
    row = lax.broadcasted_iota(I32, (tq, tk), 0)
    col = lax.broadcasted_iota(I32, (tq, tk), 1)
    rel = col - half - row
    band = jnp.abs(rel) <= half
    absrel = jnp.abs(rel).astype(F32) * float(dilation)
    lane = lax.broadcasted_iota(I32, (tq, LANES), 1)
    scale = HEAD_DIM ** -0.5

    def window(p_ref, c_ref, n_ref, gg, j, sl):
        r0 = gg * w + j * tq
        hrow = slice(gg * half, (gg + 1) * half) if g > 1 else slice(None)
        before = p_ref[hrow, sl] if j == 0 else c_ref[r0 - half:r0, sl]
        after = n_ref[hrow, sl] if j == nsub - 1 else c_ref[r0 + tq:r0 + tq + half, sl]
        return jnp.concatenate([before, c_ref[r0:r0 + tq, sl], after], axis=0)

    blocks = [(gg, j) for gg in range(g) for j in range(nsub)]
    groups = [blocks] if g > 1 else [[blk] for blk in blocks]
    for group in groups:
        items = []
        for gg, j in group:
            r0 = gg * w + j * tq
            key = t * w + j * tq - half + col
            valid = band & (key >= 0) & (key < sub_len)
            for h in range(N_HEADS):
                sl = slice(h * HEAD_DIM, (h + 1) * HEAD_DIM)
                sc = _dot_nt(q_ref[r0:r0 + tq, sl], window(kp_ref, kc_ref, kn_ref, gg, j, sl))
                items.append(dict(gg=gg, j=j, h=h, sl=sl, r0=r0, valid=valid, sc=sc))
        for it in items:
            slope = 2.0 ** (-8.0 * (it["h"] + 1) / N_HEADS)
            sc = jnp.where(it["valid"], it["sc"] * scale - slope * absrel, NEG_INF)
            m = jnp.max(sc, axis=-1, keepdims=True)
            p = jnp.exp(sc - m)
            den = jnp.sum(p, axis=-1, keepdims=True)
            it["p"], it["den"], it["lse"] = p.astype(BF16), den, m + jnp.log(den)
        for it in items:
            pv = _dot(it["p"], window(vp_ref, vc_ref, vn_ref, it["gg"], it["j"], it["sl"]))
            o_scr[it["h"], it["r0"]:it["r0"] + tq, :] = pv / it["den"]
        for gg, j in group:
            lse_tile = jnp.zeros((tq, LANES), F32)
            for it in items:
                if (it["gg"], it["j"]) == (gg, j):
                    lse_tile = jnp.where(lane == it["h"], it["lse"], lse_tile)
            r0 = gg * w + j * tq
            l_scr[r0:r0 + tq, :] = lse_tile
    step = ATTN_PERM // dilation
    nblk = o_ref.shape[0]
    for blk in range(nblk):
        if step == 1:
            rows = slice(blk * half, (blk + 1) * half)
        else:
            rows = pl.ds(blk, half, stride=step)
        for h in range(N_HEADS):
            o_ref[blk, :, h * HEAD_DIM:(h + 1) * HEAD_DIM] = o_scr[h, rows, :].astype(o_ref.dtype)
        lse_ref[blk] = l_scr[rows, :]


def dilated_attention(proj, gi, dilation):
    b, s, _ = proj.shape
    d = D_MODEL
    half = ATTN_HALF
    tp = ATTN_TP
    assert s % tp == 0 and tp % (dilation * half) == 0 and ATTN_PERM % dilation == 0
    nt = s // tp
    w = tp // dilation
    g = 1 if w > half else ATTN_RES_PER_STEP
    nres = dilation // g
    base = gi * 3
    hpt = tp // half

    def cur(which):
        return pl.BlockSpec((None, g * w, d), lambda bi, t, r: (bi, t * nres + r, base + which))

    if g == 1:
        wh = w // half

        def prev(which):
            return pl.BlockSpec((None, half, d),
                                lambda bi, t, r: (bi, jnp.maximum((t - 1) * hpt + (r + 1) * wh - 1, 0), base + which))

        def nxt(which):
            return pl.BlockSpec((None, half, d),
                                lambda bi, t, r: (bi, jnp.minimum(t + 1, nt - 1) * hpt + r * wh, base + which))
    else:
        def prev(which):
            return pl.BlockSpec((None, g * w, d), lambda bi, t, r: (bi, jnp.maximum(t - 1, 0) * nres + r, base + which))

        def nxt(which):
            return pl.BlockSpec((None, g * w, d),
                                lambda bi, t, r: (bi, jnp.minimum(t + 1, nt - 1) * nres + r, base + which))

    perm = ATTN_PERM
    if dilation == perm:
        nblk, lead = g, (b, nt, perm)
        omap = lambda bi, t, r: (bi, t, r, 0, 0)
        oblock = lambda last: (None, None, nblk, half, last)
    else:
        nblk, lead = perm // dilation, (b, nt, perm // dilation, dilation)
        omap = lambda bi, t, r: (bi, t, 0, r, 0, 0)
        oblock = lambda last: (None, None, nblk, None, half, last)

    o, lse = pl.pallas_call(
        functools.partial(_attn_kernel, dilation=dilation, w=w, g=g, sub_len=s // dilation),
        grid=(b, nt, nres),
        in_specs=[cur(0), prev(1), cur(1), nxt(1), prev(2), cur(2), nxt(2)],
        out_specs=[pl.BlockSpec(oblock(d), omap), pl.BlockSpec(oblock(LANES), omap)],
        out_shape=[jax.ShapeDtypeStruct(lead + (half, d), BF16), jax.ShapeDtypeStruct(lead + (half, LANES), F32)],
        scratch_shapes=[pltpu.VMEM((N_HEADS, g * w, HEAD_DIM), F32), pltpu.VMEM((g * w, LANES), F32)],
        compiler_params=_cparams("parallel", "parallel", "parallel"),
        name=f"dilated_attn_g{gi}",
    )(proj, proj, proj, proj, proj, proj, proj)
    return o.reshape(b * s, d), lse.reshape(b * s, LANES)


def _attn_out_kernel(o0_ref, o1_ref, o2_ref, l0_ref, l1_ref, l2_ref, wo_ref, x_ref, out_ref, y_scr):
    parts = []
    for h in range(N_HEADS):
        sl = slice(h * HEAD_DIM, (h + 1) * HEAD_DIM)
        a0, a1, a2 = l0_ref[:, h:h + 1], l1_ref[:, h:h + 1], l2_ref[:, h:h + 1]
        mx = jnp.maximum(jnp.maximum(a0, a1), a2)
        e0, e1, e2 = jnp.exp(a0 - mx), jnp.exp(a1 - mx), jnp.exp(a2 - mx)
        o = (e0 * o0_ref[:, sl].astype(F32) + e1 * o1_ref[:, sl].astype(F32) + e2 * o2_ref[:, sl].astype(F32))
        parts.append((o / (e0 + e1 + e2)).astype(BF16))
    y = _dot(jnp.concatenate(parts, axis=1), wo_ref[...])
    ncol = y_scr.shape[0]
    for c in range(ncol):
        y_scr[c] = y[:, c * LANES:(c + 1) * LANES]
    per = ATTN_TP // ATTN_PERM
    for l in range(per):
        rows = slice(l * ATTN_PERM, (l + 1) * ATTN_PERM)
        for c in range(ncol):
            cols = slice(c * LANES, (c + 1) * LANES)
            out_ref[rows, cols] = x_ref[rows, cols] + y_scr[c, pl.ds(l, ATTN_PERM, stride=per), :]


def attn_out(os, lses, w_out, x):
    n, d = x.shape
    tm = ATTN_TP
    tile = lambda i: (i, 0)
    big = pl.BlockSpec((tm, d), tile)
    small = pl.BlockSpec((tm, LANES), tile)
    return pl.pallas_call(
        _attn_out_kernel,
        grid=(n // tm,),
        in_specs=[big, big, big, small, small, small, pl.BlockSpec((d, d), lambda i: (0, 0)), big],
        out_specs=big,
        out_shape=jax.ShapeDtypeStruct((n, d), F32),
        scratch_shapes=[pltpu.VMEM((d // LANES, tm, LANES), F32)],
        compiler_params=_cparams("parallel"),
        name="attn_out",
    )(*os, *lses, w_out, x)


def _attn_proj_kernel(x_ref, g_ref, w_ref, o_ref, h_ref, acc_ref, *, blocks_per_group):
    j = pl.program_id(1)

    @pl.when(j == 0)
    def _():
        h_ref[...] = _rms(x_ref[...], g_ref[...]).astype(BF16)

    res = _dot(h_ref[...], w_ref[...])
    ncol = acc_ref.shape[0]
    for c in range(ncol):
        acc_ref[c] = res[:, c * LANES:(c + 1) * LANES]
    gi = j // blocks_per_group
    for gidx, (_, dilation) in enumerate(ATTN_GROUPS):
        @pl.when(gi == gidx)
        def _():
            if dilation == 1:
                o_ref[...] = res.astype(o_ref.dtype)
            else:
                w = ATTN_TP // dilation
                for r in range(dilation):
                    for c in range(ncol):
                        o_ref[r * w:(r + 1) * w, c * LANES:(c + 1) * LANES] = (
                            acc_ref[c, pl.ds(r, w, stride=dilation), :].astype(o_ref.dtype))


def attn_proj(x, g, w, tn=1024):
    n, d = x.shape
    nout = w.shape[1]
    tm = ATTN_TP
    group_cols = nout // len(ATTN_GROUPS)
    assert group_cols % tn == 0
    return pl.pallas_call(
        functools.partial(_attn_proj_kernel, blocks_per_group=group_cols // tn),
        grid=(n // tm, nout // tn),
        in_specs=[pl.BlockSpec((tm, d), lambda i, j: (i, 0)),
                  pl.BlockSpec((1, d), lambda i, j: (0, 0)),
                  pl.BlockSpec((d, tn), lambda i, j: (0, j))],
        out_specs=pl.BlockSpec((tm, tn), lambda i, j: (i, j)),
        out_shape=jax.ShapeDtypeStruct((n, nout), BF16),
        scratch_shapes=[pltpu.VMEM((tm, d), BF16), pltpu.VMEM((tn // LANES, tm, LANES), F32)],
        compiler_params=_cparams("parallel", "arbitrary"),
        name="attn_proj",
    )(x, g.reshape(1, d), w)


def _route_kernel(x_ref, g_ref, r_ref, h_ref, aff_ref):
    h = _rms(x_ref[...], g_ref[...])
    h_ref[...] = h.astype(BF16)
    logits = jnp.dot(h, r_ref[...], precision=lax.Precision.HIGHEST, preferred_element_type=F32)
    lt = logits.T[:N_EXPERTS, :]
    m = jnp.max(lt, axis=0, keepdims=True)
    e = jnp.exp(lt - m)
    aff_ref[...] = e / jnp.sum(e, axis=0, keepdims=True)


def moe_route(x, g, router, tm=512):
    n, d = x.shape
    rpad = jnp.zeros((d, LANES), F32).at[:, :N_EXPERTS].set(router)
    return pl.pallas_call(
        _route_kernel,
        grid=(n // tm,),
        in_specs=[pl.BlockSpec((tm, d), lambda i: (i, 0)),
                  pl.BlockSpec((1, d), lambda i: (0, 0)),
                  pl.BlockSpec((d, LANES), lambda i: (0, 0))],
        out_specs=[pl.BlockSpec((tm, d), lambda i: (i, 0)),
                   pl.BlockSpec((N_EXPERTS, tm), lambda i: (0, i))],
        out_shape=[jax.ShapeDtypeStruct((n, d), BF16), jax.ShapeDtypeStruct((N_EXPERTS, n), F32)],
        compiler_params=_cparams("parallel"),
        name="moe_route",
    )(x, g.reshape(1, d), rpad)


def _select_kernel(aff_ref, slot_ref, tab_ref, *, cap):
    e, n = aff_ref.shape
    bits = pltpu.bitcast(aff_ref[...], I32)
    tidx = lax.broadcasted_iota(I32, (e, n), 1)

    def count(mask_i32):
        return jnp.sum(mask_i32.astype(F32), axis=1, keepdims=True).astype(I32)

    def thr_body(it, thr):
        cand = thr | jnp.left_shift(jnp.int32(1), 30 - it)
        cnt = count(jnp.where(bits >= cand, 1, 0))
        return jnp.where(cnt >= cap, cand, thr)

    thr = lax.fori_loop(0, 31, thr_body, jnp.zeros((e, 1), I32))
    gt = jnp.where(bits > thr, 1, 0)
    eq = jnp.where(bits == thr, 1, 0)
    need = cap - count(gt)
    nbits = max(1, (n - 1).bit_length())

    def tie_body(it, j):
        cand = j | jnp.left_shift(jnp.int32(1), nbits - 1 - it)
        cnt = count(jnp.where(tidx < cand, eq, 0))
        return jnp.where(cnt < need, cand, j)

    jlast = lax.fori_loop(0, nbits, tie_body, jnp.zeros((e, 1), I32))
    sel = gt + jnp.where(tidx <= jlast, eq, 0)
    slot_ref[...] = sel
    rr = lax.broadcasted_iota(I32, (LANES, LANES), 0)
    cc = lax.broadcasted_iota(I32, (LANES, LANES), 1)
    upper = jnp.where(rr <= cc, 1.0, 0.0).astype(BF16)

    nblk = n // LANES
    blk_lane = lax.broadcasted_iota(I32, (e, nblk), 1)

    tab_ref[...] = jnp.zeros((e, nblk), I32)

    def blk_body(bi, run):
        off = pl.multiple_of(bi * LANES, LANES)
        sb = slot_ref[:, pl.ds(off, LANES)]
        inc = _dot(sb.astype(F32).astype(BF16), upper).astype(I32)
        slot_ref[:, pl.ds(off, LANES)] = jnp.where(sb > 0, inc - sb + run, -1)
        tab_ref[...] = jnp.where(blk_lane == bi, run, tab_ref[...])
        return run + inc[:, LANES - 1:LANES]

    lax.fori_loop(0, nblk, blk_body, jnp.zeros((e, 1), I32))


def moe_select(aff_t, cap):
    e, n = aff_t.shape
    return pl.pallas_call(
        functools.partial(_select_kernel, cap=cap),
        grid=(1,),
        in_specs=[pl.BlockSpec((e, n), lambda i: (0, 0))],
        out_specs=[pl.BlockSpec((e, n), lambda i: (0, 0)), pl.BlockSpec((e, n // LANES), lambda i: (0, 0))],
        out_shape=[jax.ShapeDtypeStruct((e, n), I32), jax.ShapeDtypeStruct((e, n // LANES), I32)],
        compiler_params=_cparams("arbitrary"),
        name="moe_select",
    )(aff_t)


def _ffn_kernel(x_ref, wg_ref, wu_ref, wd_ref, o_ref):
    x = x_ref[...]
    acc = None
    for f in range(EXPERT_FF // FF_CHUNK):
        fs = slice(f * FF_CHUNK, (f + 1) * FF_CHUNK)
        hid = (_silu(_dot(x, wg_ref[:, fs])) * _dot(x, wu_ref[:, fs])).astype(BF16)
        part = _dot(hid, wd_ref[fs, :])
        acc = part if acc is None else acc + part
    o_ref[...] = acc.astype(o_ref.dtype)


def moe_ffn(xe, w_gate, w_up, w_down, layer, tm=1024):
    e, c, d = xe.shape
    f = w_gate.shape[3]
    return pl.pallas_call(
        _ffn_kernel,
        grid=(e, c // tm),
        in_specs=[pl.BlockSpec((None, tm, d), lambda ei, j: (ei, j, 0)),
                  pl.BlockSpec((None, None, d, f), lambda ei, j: (layer, ei, 0, 0)),
                  pl.BlockSpec((None, None, d, f), lambda ei, j: (layer, ei, 0, 0)),
                  pl.BlockSpec((None, None, f, d), lambda ei, j: (layer, ei, 0, 0))],
        out_specs=pl.BlockSpec((None, tm, d), lambda ei, j: (ei, j, 0)),
        out_shape=jax.ShapeDtypeStruct((e, c, d), BF16),
        compiler_params=_cparams("parallel", "arbitrary"),
        name="moe_ffn",
    )(xe, w_gate, w_up, w_down)


def _slot_ranges(tab_ref, i, cap):
    out = []
    for e in range(N_EXPERTS):
        p0 = tab_ref[i * N_EXPERTS + e]
        out.append((e * cap + p0, tab_ref[(i + 1) * N_EXPERTS + e] - p0))
    return out


def _dispatch_kernel(tab_ref, h_ref, slot_ref, xe_ref, stage_ref, tail_ref, sem, npiece_ref, *, cap, tile):
    i = pl.program_id(0)
    last = pl.num_programs(0) - 1
    slot = i % 2
    grp = BF16_SUBLANES
    kc = COMBINE_KCHUNK

    def drain(sl):
        def body(_, carry):
            pltpu.make_async_copy(stage_ref.at[sl, pl.ds(0, grp)], xe_ref.at[pl.ds(0, grp)], sem.at[sl]).wait()
            return carry
        lax.fori_loop(0, npiece_ref[sl], body, 0)

    @pl.when(i >= 2)
    def _():
        drain(slot)

    meta, bases = [], []
    off = jnp.int32(0)
    for g0, cnt in _slot_ranges(tab_ref, i, cap):
        a = (g0 // grp) * grp
        lead = g0 - a
        tot = lead + cnt
        meta.append((off, a, lead, tot, (tot // grp) * grp))
        bases.append(off - a)
        off = off + ((tot + grp - 1) // grp) * grp
    row_e = lax.broadcasted_iota(I32, (N_EXPERTS, 1), 0)
    cvec = jnp.zeros((N_EXPERTS, 1), I32)
    for e in range(N_EXPERTS):
        cvec = jnp.where(row_e == e, bases[e] + e * cap, cvec)
    slot_e = slot_ref[...]
    col = jnp.where(slot_e >= 0, slot_e + cvec, -1)

    def chunk_body(ci, carry):
        base = pl.multiple_of(ci * kc, kc)
        rows = lax.broadcasted_iota(I32, (kc, tile), 0) + base
        p = jnp.zeros((kc, tile), F32)
        for e in range(N_EXPERTS):
            p = jnp.where(col[e:e + 1, :] == rows, 1.0, p)
        stage_ref[slot, pl.ds(base, kc), :] = _dot(p.astype(BF16), h_ref[...]).astype(stage_ref.dtype)
        return carry

    lax.fori_loop(0, (off + kc - 1) // kc, chunk_body, 0)

    row_g = lax.broadcasted_iota(I32, (grp, D_MODEL), 0)
    npiece = jnp.int32(0)
    for e, (off_e, a, lead, tot, nfull) in enumerate(meta):
        first = pl.ds(pl.multiple_of(off_e, grp), grp)

        @pl.when(lead > 0)
        def _():
            stage_ref[slot, first, :] = jnp.where(row_g < lead, tail_ref[e], stage_ref[slot, first, :])

        @pl.when(tot > nfull)
        def _():
            tail_ref[e] = stage_ref[slot, pl.ds(pl.multiple_of(off_e + nfull, grp), grp), :]

        def piece(k, carry):
            pltpu.make_async_copy(stage_ref.at[slot, pl.ds(pl.multiple_of(off_e + k * grp, grp), grp)],
                                  xe_ref.at[pl.ds(pl.multiple_of(a + k * grp, grp), grp)], sem.at[slot]).start()
            return carry

        lax.fori_loop(0, nfull // grp, piece, 0)
        npiece = npiece + nfull // grp
    npiece_ref[slot] = npiece

    @pl.when(i == last)
    def _():
        drain(slot)

        @pl.when(i >= 1)
        def _():
            drain(1 - slot)


def moe_dispatch(hb, slot, tab, cap, tile=COMBINE_TILE):
    n, d = hb.shape
    kmax = N_EXPERTS * (tile + BF16_SUBLANES)
    kmax = -(-kmax // COMBINE_KCHUNK) * COMBINE_KCHUNK
    grid_spec = pltpu.PrefetchScalarGridSpec(
        num_scalar_prefetch=1,
        grid=(n // tile,),
        in_specs=[pl.BlockSpec((tile, d), lambda i, tab: (i, 0)),
                  pl.BlockSpec((N_EXPERTS, tile), lambda i, tab: (0, i))],
        out_specs=pl.BlockSpec(memory_space=pl.ANY),
        scratch_shapes=[pltpu.VMEM((2, kmax, d), BF16), pltpu.VMEM((N_EXPERTS, BF16_SUBLANES, d), BF16),
                        pltpu.SemaphoreType.DMA((2,)), pltpu.SMEM((2,), I32)],
    )
    return pl.pallas_call(
        functools.partial(_dispatch_kernel, cap=cap, tile=tile),
        grid_spec=grid_spec,
        out_shape=jax.ShapeDtypeStruct((N_EXPERTS * cap, d), BF16),
        compiler_params=_cparams("arbitrary"),
        name="moe_dispatch",
    )(tab, hb, slot)


def _combine_ranges(tab_ref, i, cap):
    grp = BF16_SUBLANES
    slabs, bases = [], []
    off = jnp.int32(0)
    for e, (g0, cnt) in enumerate(_slot_ranges(tab_ref, i, cap)):
        a = (g0 // grp) * grp
        nrow = jnp.where(cnt > 0, ((g0 - a + cnt + grp - 1) // grp) * grp, 0)
        slabs.append((off, a, nrow))
        bases.append(off - a + e * cap)
        off = off + nrow
    return slabs, bases, off


def _combine_copies(tab_ref, ye_ref, zbuf_ref, sem, i, cap, tile):
    grp = BF16_SUBLANES
    max_shift = (tile // grp + 1).bit_length() - 1
    copies = []
    for off_e, a, nrow in _combine_ranges(tab_ref, i, cap)[0]:
        src, dst = a, off_e
        for sh in range(max_shift, -1, -1):
            size = grp << sh
            take = (nrow & size) != 0
            copies.append((take, pltpu.make_async_copy(ye_ref.at[pl.ds(pl.multiple_of(src, grp), size)],
                                                       zbuf_ref.at[pl.ds(pl.multiple_of(dst, grp), size)], sem)))
            step = jnp.where(take, size, 0)
            src, dst = src + step, dst + step
    return copies


def _combine_kernel(tab_ref, x_ref, slot_ref, aff_ref, ye_ref, g_ref, o_ref, zbuf_ref, sem,
                    *, cap, tile, final_norm):
    i = pl.program_id(0)
    slot = i % 2

    def start_tile(ti, sl):
        for take, cp in _combine_copies(tab_ref, ye_ref, zbuf_ref.at[sl], sem.at[sl], ti, cap, tile):
            @pl.when(take)
            def _():
                cp.start()

    @pl.when(i == 0)
    def _():
        zbuf_ref[...] = jnp.zeros_like(zbuf_ref)
        start_tile(0, 0)

    @pl.when(i + 1 < pl.num_programs(0))
    def _():
        start_tile(i + 1, 1 - slot)

    zcur_ref = zbuf_ref.at[slot]
    _, bases, k_tot = _combine_ranges(tab_ref, i, cap)
    row_e = lax.broadcasted_iota(I32, (N_EXPERTS, 1), 0)
    cvec = jnp.zeros((N_EXPERTS, 1), I32)
    for e in range(N_EXPERTS):
        cvec = jnp.where(row_e == e, bases[e], cvec)
    slot_e = slot_ref[...]
    col = jnp.where(slot_e >= 0, slot_e + cvec, -1)
    aff = aff_ref[...]
    o_ref[...] = x_ref[...]

    for take, cp in _combine_copies(tab_ref, ye_ref, zcur_ref, sem.at[slot], i, cap, tile):
        @pl.when(take)
        def _():
            cp.wait()

    kc = COMBINE_KCHUNK

    def chunk_body(ci, carry):
        base = pl.multiple_of(ci * kc, kc)
        z = zcur_ref[pl.ds(base, kc), :]
        rows = lax.broadcasted_iota(I32, (kc, tile), 0) + base
        p = jnp.zeros((kc, tile), F32)
        for e in range(N_EXPERTS):
            p = jnp.where(col[e:e + 1, :] == rows, aff[e:e + 1, :], p)
        p_hi = p.astype(BF16)
        p_lo = (p - p_hi.astype(F32)).astype(BF16)
        o_ref[...] += _dot_tn(p_hi, z) + _dot_tn(p_lo, z)
        return carry

    lax.fori_loop(0, (k_tot + kc - 1) // kc, chunk_body, 0)
    if final_norm:
        o_ref[...] = _rms(o_ref[...], g_ref[...])


def moe_combine(x, slot, aff, tab, ye, cap, norm_g=None, tile=COMBINE_TILE):
    n, d = x.shape
    kmax = N_EXPERTS * (tile + BF16_SUBLANES)
    kmax = -(-kmax // COMBINE_KCHUNK) * COMBINE_KCHUNK
    final_norm = norm_g is not None
    g = (norm_g if final_norm else jnp.ones((d,), F32)).reshape(1, d)
    grid_spec = pltpu.PrefetchScalarGridSpec(
        num_scalar_prefetch=1,
        grid=(n // tile,),
        in_specs=[pl.BlockSpec((tile, d), lambda i, tab: (i, 0)),
                  pl.BlockSpec((N_EXPERTS, tile), lambda i, tab: (0, i)),
                  pl.BlockSpec((N_EXPERTS, tile), lambda i, tab: (0, i)),
                  pl.BlockSpec(memory_space=pl.ANY),
                  pl.BlockSpec((1, d), lambda i, tab: (0, 0))],
        out_specs=pl.BlockSpec((tile, d), lambda i, tab: (i, 0)),
        scratch_shapes=[pltpu.VMEM((2, kmax, d), BF16), pltpu.SemaphoreType.DMA((2,))],
    )
    return pl.pallas_call(
        functools.partial(_combine_kernel, cap=cap, tile=tile, final_norm=final_norm),
        grid_spec=grid_spec,
        out_shape=jax.ShapeDtypeStruct((n, d), F32),
        compiler_params=_cparams("arbitrary"),
        name="moe_combine",
    )(tab, x, slot, aff, ye, g)


def expert_choice_moe(x, norm_g, router, w_gate, w_up, w_down, layer, final_g=None):
    n, d = x.shape
    cap = max(1, CAPACITY_FACTOR * n // N_EXPERTS)
    hb, aff_t = moe_route(x, norm_g, router)
    slot, blk_tab = moe_select(aff_t, cap)
    tab = jnp.concatenate([blk_tab[:, ::COMBINE_TILE // LANES].T, jnp.full((1, N_EXPERTS), cap, I32)], axis=0)
    tab = tab.reshape(-1)
    xe = moe_dispatch(hb, slot, tab, cap).reshape(N_EXPERTS, cap, d)
    ye = moe_ffn(xe, w_gate, w_up, w_down, layer).reshape(N_EXPERTS * cap, d)
    return moe_combine(x, slot, aff_t, tab, ye, cap, norm_g=final_g)


def _trunk(x, p):
    b, s, d = x.shape
    n = b * s
    x = x.reshape(n, d)
    for i in range(DEPTH):
        kind, j = i % N_MIXERS, i // N_MIXERS
        if kind == 0:
            u = norm_glu(x, p["norm_mix"][i], p["conv_w_glu"][j], p["conv_b_glu"][j])
            x = conv_out(u.reshape(b, s, d), x.reshape(b, s, d), p["conv_w_dw"][j], p["conv_b_dw"][j],
                         p["conv_ln_g"][j], p["conv_ln_b"][j], p["conv_w_out"][j], p["conv_b_out"][j]).reshape(n, d)
        elif kind == 1:
            proj = norm_linear(x, p["norm_mix"][i], p["dn_w_in"][j], F32, tn=768)
            q, k, v, bg = dn_prep(proj.reshape(b, s, -1), p["dn_w_conv"][j], p["dn_a_log"][j], p["dn_dt_bias"][j])
            o_f, o_b = delta_rule(q, k, v, bg)
            x = dn_out(o_f.reshape(n, d), o_b.reshape(n, d), proj, p["dn_norm_g"][j], p["dn_w_out"][j], x)
        else:
            proj = attn_proj(x, p["norm_mix"][i], p["attn_w_in"][j]).reshape(b, s, -1)
            os, lses = [], []
            for gi, (window, dilation) in enumerate(ATTN_GROUPS):
                assert window // (2 * dilation) == ATTN_HALF
                o, lse = dilated_attention(proj, gi, dilation)
                os.append(o)
                lses.append(lse)
            x = attn_out(os, lses, p["attn_w_out"][j], x)
        x = expert_choice_moe(x, p["norm_ffn"][i], p["moe_router"][i], p["moe_w_gate"], p["moe_w_up"],
                              p["moe_w_down"], i, final_g=p["norm_final"] if i == DEPTH - 1 else None)
    return x.reshape(b, s, d)


def kernel(x_prompt, x_sample, norm_mix, norm_ffn, norm_final, conv_w_glu, conv_b_glu, conv_w_dw, conv_b_dw,
           conv_ln_g, conv_ln_b, conv_w_out, conv_b_out, dn_w_in, dn_w_conv, dn_a_log, dn_dt_bias, dn_norm_g,
           dn_w_out, attn_w_in, attn_w_out, moe_router, moe_w_gate, moe_w_up, moe_w_down):
    dn_pad = (-dn_w_in.shape[-1]) % 512
    p = dict(
        norm_mix=norm_mix, norm_ffn=norm_ffn, norm_final=norm_final,
        conv_w_glu=conv_w_glu.astype(BF16), conv_b_glu=conv_b_glu, conv_w_dw=conv_w_dw, conv_b_dw=conv_b_dw,
        conv_ln_g=conv_ln_g, conv_ln_b=conv_ln_b, conv_w_out=conv_w_out.astype(BF16), conv_b_out=conv_b_out,
        dn_w_in=jnp.pad(dn_w_in, ((0, 0), (0, 0), (0, dn_pad))).astype(BF16), dn_w_conv=dn_w_conv,
        dn_a_log=dn_a_log, dn_dt_bias=dn_dt_bias, dn_norm_g=dn_norm_g, dn_w_out=dn_w_out.astype(BF16),
        attn_w_in=attn_w_in.astype(BF16), attn_w_out=attn_w_out.astype(BF16),
        moe_router=moe_router, moe_w_gate=moe_w_gate.astype(BF16), moe_w_up=moe_w_up.astype(BF16),
        moe_w_down=moe_w_down.astype(BF16),
    )
    return (_trunk(x_prompt, p), _trunk(x_sample, p))
```

```python
import functools

import jax
import jax.numpy as jnp
from jax import lax
from jax.experimental import pallas as pl
from jax.experimental.pallas import tpu as pltpu

F32 = jnp.float32
BF16 = jnp.bfloat16
I32 = jnp.int32

D_MODEL = 1024
DEPTH = 4
N_MIXERS = 3
RMS_EPS = 1e-6
LN_EPS = 1e-5
NEG_INF = -1e30
CONV_WIDTH = 31
HEAD_DIM = 128
N_HEADS = D_MODEL // HEAD_DIM
SHORT_CONV = 5
ATTN_GROUPS = ((128, 1), (512, 4), (2048, 16))
N_EXPERTS = 16
EXPERT_FF = 2 * D_MODEL
CAPACITY_FACTOR = 2

VMEM_LIMIT_BYTES = 56 * 1024 * 1024
LANES = 128
BF16_SUBLANES = 16

DN_CHUNK = 64
ATTN_HALF = 64
ATTN_TQ = 128
ATTN_PERM = 16
ATTN_TP = ATTN_PERM * ATTN_HALF
ATTN_RES_PER_STEP = 4
FF_CHUNK = 512
COMBINE_TILE = 256
COMBINE_KCHUNK = 256


def _cparams(*sem):
    return pltpu.CompilerParams(dimension_semantics=sem, vmem_limit_bytes=VMEM_LIMIT_BYTES)


def _rms(x, g):
    return x * lax.rsqrt(jnp.mean(x * x, axis=-1, keepdims=True) + RMS_EPS) * g


def _silu(x):
    return x * jax.nn.sigmoid(x)


def _dot(a, b):
    return jnp.dot(a, b, preferred_element_type=F32)


def _dot_nt(a, b):
    return lax.dot_general(a, b, (((1,), (1,)), ((), ())), preferred_element_type=F32)


def _dot_tn(a, b):
    return lax.dot_general(a, b, (((0,), (0,)), ((), ())), preferred_element_type=F32)


def _norm_linear_kernel(x_ref, g_ref, w_ref, o_ref, h_ref):
    @pl.when(pl.program_id(1) == 0)
    def _():
        h_ref[...] = _rms(x_ref[...], g_ref[...]).astype(BF16)

    o_ref[...] = _dot(h_ref[...], w_ref[...]).astype(o_ref.dtype)


def norm_linear(x, g, w, out_dtype, tm=1024, tn=512):
    n, d = x.shape
    nout = w.shape[1]
    return pl.pallas_call(
        _norm_linear_kernel,
        grid=(n // tm, nout // tn),
        in_specs=[pl.BlockSpec((tm, d), lambda i, j: (i, 0)),
                  pl.BlockSpec((1, d), lambda i, j: (0, 0)),
                  pl.BlockSpec((d, tn), lambda i, j: (0, j))],
        out_specs=pl.BlockSpec((tm, tn), lambda i, j: (i, j)),
        out_shape=jax.ShapeDtypeStruct((n, nout), out_dtype),
        scratch_shapes=[pltpu.VMEM((tm, d), BF16)],
        compiler_params=_cparams("parallel", "arbitrary"),
        name="norm_linear",
    )(x, g.reshape(1, d), w)


def _norm_glu_kernel(x_ref, g_ref, wa_ref, wg_ref, ba_ref, bg_ref, o_ref, h_ref):
    @pl.when(pl.program_id(1) == 0)
    def _():
        h_ref[...] = _rms(x_ref[...], g_ref[...]).astype(BF16)

    h = h_ref[...]
    a = _dot(h, wa_ref[...]) + ba_ref[...]
    gate = _dot(h, wg_ref[...]) + bg_ref[...]
    o_ref[...] = a * jax.nn.sigmoid(gate)


def norm_glu(x, g, w, b, tm=1024, tn=1024):
    n, d = x.shape
    nb = d // tn
    b2 = b.reshape(1, 2 * d)
    return pl.pallas_call(
        _norm_glu_kernel,
        grid=(n // tm, nb),
        in_specs=[pl.BlockSpec((tm, d), lambda i, j: (i, 0)),
                  pl.BlockSpec((1, d), lambda i, j: (0, 0)),
                  pl.BlockSpec((d, tn), lambda i, j: (0, j)),
                  pl.BlockSpec((d, tn), lambda i, j: (0, j + nb)),
                  pl.BlockSpec((1, tn), lambda i, j: (0, j)),
                  pl.BlockSpec((1, tn), lambda i, j: (0, j + nb))],
        out_specs=pl.BlockSpec((tm, tn), lambda i, j: (i, j)),
        out_shape=jax.ShapeDtypeStruct((n, d), F32),
        scratch_shapes=[pltpu.VMEM((tm, d), BF16)],
        compiler_params=_cparams("parallel", "arbitrary"),
        name="norm_glu",
    )(x, g.reshape(1, d), w, w, b2, b2)


CONV_HALO = 16


def _conv_out_kernel(up_ref, uc_ref, un_ref, wdw_ref, bdw_ref, lng_ref, lnb_ref, wo_ref, bo_ref, x_ref,
                     o_ref, win_ref, shift_ref, *, ts):
    i = pl.program_id(1)
    last = pl.num_programs(1) - 1
    win_ref[0:CONV_HALO, :] = jnp.where(i > 0, up_ref[...], 0.0)
    win_ref[CONV_HALO:CONV_HALO + ts, :] = uc_ref[...]
    win_ref[CONV_HALO + ts:, :] = jnp.where(i < last, un_ref[...], 0.0)
    for s in range(1, 8):
        shift_ref[s - 1] = win_ref[pl.ds(s, ts + 24), :]
    acc = jnp.zeros((ts, D_MODEL), F32) + bdw_ref[...]
    for s in range(8):
        for a in range(4):
            j = 8 * a + s - 1
            if 0 <= j < CONV_WIDTH:
                rows = slice(8 * a, 8 * a + ts)
                tap = win_ref[rows, :] if s == 0 else shift_ref[s - 1, rows, :]
                acc = acc + wdw_ref[j:j + 1, :] * tap
    mu = jnp.mean(acc, axis=-1, keepdims=True)
    cen = acc - mu
    var = jnp.mean(cen * cen, axis=-1, keepdims=True)
    u = cen * lax.rsqrt(var + LN_EPS) * lng_ref[...] + lnb_ref[...]
    y = _silu(u).astype(BF16)
    o_ref[...] = x_ref[...] + _dot(y, wo_ref[...]) + bo_ref[...]


def conv_out(u, x, w_dw, b_dw, ln_g, ln_b, w_out, b_out, ts=256):
    b, s, d = u.shape
    hb = ts // CONV_HALO
    nblk = s // CONV_HALO
    wdw = jnp.concatenate([w_dw, jnp.zeros((1, d), F32)], axis=0)
    row = lambda v: v.reshape(1, d)
    return pl.pallas_call(
        functools.partial(_conv_out_kernel, ts=ts),
        grid=(b, s // ts),
        in_specs=[pl.BlockSpec((None, CONV_HALO, d), lambda bi, i: (bi, jnp.maximum(i * hb - 1, 0), 0)),
                  pl.BlockSpec((None, ts, d), lambda bi, i: (bi, i, 0)),
                  pl.BlockSpec((None, CONV_HALO, d), lambda bi, i: (bi, jnp.minimum((i + 1) * hb, nblk - 1), 0)),
                  pl.BlockSpec((CONV_WIDTH + 1, d), lambda bi, i: (0, 0)),
                  pl.BlockSpec((1, d), lambda bi, i: (0, 0)),
                  pl.BlockSpec((1, d), lambda bi, i: (0, 0)),
                  pl.BlockSpec((1, d), lambda bi, i: (0, 0)),
                  pl.BlockSpec((d, d), lambda bi, i: (0, 0)),
                  pl.BlockSpec((1, d), lambda bi, i: (0, 0)),
                  pl.BlockSpec((None, ts, d), lambda bi, i: (bi, i, 0))],
        out_specs=pl.BlockSpec((None, ts, d), lambda bi, i: (bi, i, 0)),
        out_shape=jax.ShapeDtypeStruct((b, s, d), F32),
        scratch_shapes=[pltpu.VMEM((ts + 2 * CONV_HALO, d), F32), pltpu.VMEM((7, ts + 24, d), F32)],
        compiler_params=_cparams("parallel", "parallel"),
        name="conv_out",
    )(u, u, u, wdw, row(b_dw), row(ln_g), row(ln_b), w_out, row(b_out), x)


DN_HALO = 8
DN_BG = 32


def _dn_prep_kernel(pp_ref, pc_ref, pn_ref, ba_ref, wc_ref, alog_ref, dtb_ref,
                    q_ref, k_ref, v_ref, bg_ref, win_ref, *, ts):
    i = pl.program_id(1)
    last = pl.num_programs(1) - 1
    w3 = 3 * D_MODEL
    win_ref[0:DN_HALO, :] = jnp.where(i > 0, pp_ref[...], 0.0)
    win_ref[DN_HALO:DN_HALO + ts, :] = pc_ref[...]
    win_ref[DN_HALO + ts:, :] = jnp.where(i < last, pn_ref[...], 0.0)
    left = (SHORT_CONV - 1) // 2
    acc = jnp.zeros((ts, w3), F32)
    for j in range(SHORT_CONV):
        acc = acc + wc_ref[j:j + 1, :] * win_ref[pl.ds(DN_HALO - left + j, ts), :]
    qkv = _silu(acc)

    def l2n(t):
        return t * lax.rsqrt(jnp.sum(t * t, axis=-1, keepdims=True) + RMS_EPS)

    for h in range(N_HEADS):
        sl = slice(h * HEAD_DIM, (h + 1) * HEAD_DIM)
        q_ref[:, sl] = l2n(qkv[:, sl]) * (HEAD_DIM ** -0.5)
        k_ref[:, sl] = l2n(qkv[:, D_MODEL + h * HEAD_DIM:D_MODEL + (h + 1) * HEAD_DIM])
    v_ref[...] = qkv[:, 2 * D_MODEL:]
    ba = ba_ref[:, :DN_BG]
    lane = lax.broadcasted_iota(I32, (ts, DN_BG), 1)
    beta = jax.nn.sigmoid(ba)
    g = -jnp.exp(alog_ref[...]) * jax.nn.softplus(ba + dtb_ref[...])
    bg_ref[...] = jnp.where(lane < DN_BG // 2, beta, g)


def dn_prep(proj, w_conv, a_log, dt_bias, ts=256):
    b, s, _ = proj.shape
    d = D_MODEL
    hb = ts // DN_HALO
    nblk = s // DN_HALO
    zeros16 = jnp.zeros((16,), F32)
    alog = jnp.concatenate([zeros16, a_log.reshape(-1)]).reshape(1, DN_BG)
    dtb = jnp.concatenate([zeros16, dt_bias.reshape(-1)]).reshape(1, DN_BG)
    bacol = 4 * d // LANES
    tile = lambda bi, i: (bi, i, 0)
    return pl.pallas_call(
        functools.partial(_dn_prep_kernel, ts=ts),
        grid=(b, s // ts),
        in_specs=[pl.BlockSpec((None, DN_HALO, 3 * d), lambda bi, i: (bi, jnp.maximum(i * hb - 1, 0), 0)),
                  pl.BlockSpec((None, ts, 3 * d), tile),
                  pl.BlockSpec((None, DN_HALO, 3 * d), lambda bi, i: (bi, jnp.minimum((i + 1) * hb, nblk - 1), 0)),
                  pl.BlockSpec((None, ts, LANES), lambda bi, i: (bi, i, bacol)),
                  pl.BlockSpec((SHORT_CONV, 3 * d), lambda bi, i: (0, 0)),
                  pl.BlockSpec((1, DN_BG), lambda bi, i: (0, 0)),
                  pl.BlockSpec((1, DN_BG), lambda bi, i: (0, 0))],
        out_specs=[pl.BlockSpec((None, ts, d), tile),
                   pl.BlockSpec((None, ts, d), tile),
                   pl.BlockSpec((None, ts, d), tile),
                   pl.BlockSpec((None, ts, DN_BG), tile)],
        out_shape=[jax.ShapeDtypeStruct((b, s, d), F32)] * 3 + [jax.ShapeDtypeStruct((b, s, DN_BG), F32)],
        scratch_shapes=[pltpu.VMEM((ts + 2 * DN_HALO, 3 * d), F32)],
        compiler_params=_cparams("parallel", "parallel"),
        name="dn_prep",
    )(proj, proj, proj, proj, w_conv, alog, dtb)


def _delta_kernel(qf_ref, kf_ref, vf_ref, bgf_ref, qb_ref, kb_ref, vb_ref, bgb_ref,
                  of_ref, ob_ref, state_ref, *, chunk):
    @pl.when(pl.program_id(1) == 0)
    def _():
        state_ref[...] = jnp.zeros_like(state_ref)

    c = chunk
    dh = HEAD_DIM
    ns = 2 * dh
    r = lax.broadcasted_iota(I32, (c, c), 0)
    s = lax.broadcasted_iota(I32, (c, c), 1)
    eye = jnp.where(lax.broadcasted_iota(I32, (DN_BG, DN_BG), 0) == lax.broadcasted_iota(I32, (DN_BG, DN_BG), 1),
                    1.0, 0.0)
    hi = lax.Precision.HIGHEST
    chains = []
    for di, (q_ref, k_ref, v_ref, bg_ref, o_ref) in enumerate(
            ((qf_ref, kf_ref, vf_ref, bgf_ref, of_ref), (qb_ref, kb_ref, vb_ref, bgb_ref, ob_ref))):
        reverse = di == 1
        incl = (r <= s) if reverse else (r >= s)
        strict = (r < s) if reverse else (r > s)
        bg = bg_ref[...]
        gc_cols = jnp.dot(jnp.where(incl, 1.0, 0.0), bg, precision=hi, preferred_element_type=F32)
        gc_rows = lax.dot_general(eye, gc_cols, (((1,), (1,)), ((), ())),
                                  precision=hi, preferred_element_type=F32)
        for h in range(N_HEADS):
            jb = di * N_HEADS + h
            jg = DN_BG // 2 + jb
            gc_col = gc_cols[:, jg:jg + 1]
            chains.append(dict(
                sl=slice(h * dh, (h + 1) * dh), jb=jb, q_ref=q_ref, k_ref=k_ref, v_ref=v_ref, o_ref=o_ref,
                incl=incl, strict=strict, beta=bg[:, jb:jb + 1], gc_col=gc_col, gc_row=gc_rows[jg:jg + 1, :],
                g_tot=gc_col[0:1, :] if reverse else gc_col[c - 1:c, :]))

    for ch in chains:
        q, k, v = ch["q_ref"][:, ch["sl"]], ch["k_ref"][:, ch["sl"]], ch["v_ref"][:, ch["sl"]]
        decay = jnp.exp(jnp.where(ch["incl"], ch["gc_col"] - ch["gc_row"], NEG_INF))
        kb = k * ch["beta"]
        egc = jnp.exp(ch["gc_col"])
        qk_kk = _dot_nt(jnp.concatenate([q, kb], axis=0).astype(BF16), k.astype(BF16))
        ch["a"] = jnp.where(ch["incl"], qk_kk[:c] * decay, 0.0).astype(BF16)
        m = jnp.where(ch["strict"], -qk_kk[c:] * decay, 0.0)
        ch["x"] = jnp.concatenate([v * ch["beta"], kb * egc, m], axis=1)
        ch["qd"] = (q * egc).astype(BF16)
        ch["kd"] = (k * jnp.exp(ch["g_tot"] - ch["gc_col"])).astype(BF16)

    n_levels = max(1, (c - 1).bit_length())
    for lvl in range(n_levels):
        width = ns + c if lvl + 1 < n_levels else ns
        for ch in chains:
            x = ch["x"]
            xh = x.astype(BF16)
            prod = _dot(xh[:, ns:], xh[:, :width])
            sol = x[:, :ns] + prod[:, :ns]
            ch["x"] = jnp.concatenate([sol, prod[:, ns:]], axis=1) if width > ns else sol

    for ch in chains:
        ch["state"] = state_ref[ch["jb"]]
        lhs = jnp.concatenate([ch["x"][:, dh:].astype(BF16), ch["qd"]], axis=0)
        ch["ws"] = _dot(lhs, ch["state"].astype(BF16))
    for ch in chains:
        ch["vb"] = (ch["x"][:, :dh] - ch["ws"][:c]).astype(BF16)
        ch["o_ref"][:, ch["sl"]] = ch["ws"][c:] + _dot(ch["a"], ch["vb"])
    for ch in chains:
        state_ref[ch["jb"]] = ch["state"] * jnp.exp(ch["g_tot"]) + _dot_tn(ch["kd"], ch["vb"])


def delta_rule(q, k, v, bg, chunk=DN_CHUNK):
    b, s, d = q.shape
    nc = s // chunk
    fwd = lambda bi, c: (bi, c, 0)
    bwd = lambda bi, c: (bi, nc - 1 - c, 0)
    big = lambda im: pl.BlockSpec((None, chunk, d), im)
    return pl.pallas_call(
        functools.partial(_delta_kernel, chunk=chunk),
        grid=(b, nc),
        in_specs=[big(fwd), big(fwd), big(fwd), pl.BlockSpec((None, chunk, DN_BG), fwd),
                  big(bwd), big(bwd), big(bwd), pl.BlockSpec((None, chunk, DN_BG), bwd)],
        out_specs=[big(fwd), big(bwd)],
        out_shape=[jax.ShapeDtypeStruct((b, s, d), F32)] * 2,
        scratch_shapes=[pltpu.VMEM((2 * N_HEADS, HEAD_DIM, HEAD_DIM), F32)],
        compiler_params=_cparams("parallel", "arbitrary"),
        name="delta_rule",
    )(q, k, v, bg, q, k, v, bg)


def _dn_out_kernel(of_ref, ob_ref, z_ref, ng_ref, wo_ref, x_ref, o_ref):
    parts = []
    for h in range(N_HEADS):
        sl = slice(h * HEAD_DIM, (h + 1) * HEAD_DIM)
        o = of_ref[:, sl] + ob_ref[:, sl]
        o = o * lax.rsqrt(jnp.mean(o * o, axis=-1, keepdims=True) + RMS_EPS) * ng_ref[...] * _silu(z_ref[:, sl])
        parts.append(o.astype(BF16))
    o_ref[...] = x_ref[...] + _dot(jnp.concatenate(parts, axis=1), wo_ref[...])


def dn_out(o_f, o_b, proj, norm_g, w_out, x, tm=512):
    n, d = x.shape
    tile = lambda i: (i, 0)
    return pl.pallas_call(
        _dn_out_kernel,
        grid=(n // tm,),
        in_specs=[pl.BlockSpec((tm, d), tile), pl.BlockSpec((tm, d), tile),
                  pl.BlockSpec((tm, d), lambda i: (i, 3)),
                  pl.BlockSpec((1, HEAD_DIM), lambda i: (0, 0)),
                  pl.BlockSpec((d, d), lambda i: (0, 0)),
                  pl.BlockSpec((tm, d), tile)],
        out_specs=pl.BlockSpec((tm, d), tile),
        out_shape=jax.ShapeDtypeStruct((n, d), F32),
        compiler_params=_cparams("parallel"),
        name="dn_out",
    )(o_f, o_b, proj, norm_g.reshape(1, HEAD_DIM), w_out, x)


def _attn_kernel(q_ref, kp_ref, kc_ref, kn_ref, vp_ref, vc_ref, vn_ref, o_ref, lse_ref, o_scr, l_scr,
                 *, dilation, w, g, sub_len):
    t = pl.program_id(1)
    half = ATTN_HALF
    tq = min(ATTN_TQ, w)
    nsub = w // tq
    tk = tq + 2 * half
    row = lax.broadcasted_iota(I32, (tq, tk), 0)
    col = lax.broadcasted_iota(I32, (tq, tk), 1)
    rel = col - half - row
    band = jnp.abs(rel) <= half
    absrel = jnp.abs(rel).astype(F32) * float(dilation)
    lane = lax.broadcasted_iota(I32, (tq, LANES), 1)
    scale = HEAD_DIM ** -0.5

    def window(p_ref, c_ref, n_ref, gg, j, sl):
        r0 = gg * w + j * tq
        hrow = slice(gg * half, (gg + 1) * half) if g > 1 else slice(None)
        before = p_ref[hrow, sl] if j == 0 else c_ref[r0 - half:r0, sl]
        after = n_ref[hrow, sl] if j == nsub - 1 else c_ref[r0 + tq:r0 + tq + half, sl]
        return jnp.concatenate([before, c_ref[r0:r0 + tq, sl], after], axis=0)

    blocks = [(gg, j) for gg in range(g) for j in range(nsub)]
    groups = [blocks] if g > 1 else [[blk] for blk in blocks]
    for group in groups:
        items = []
        for gg, j in group:
            r0 = gg * w + j * tq
            key = t * w + j * tq - half + col
            valid = band & (key >= 0) & (key < sub_len)
            for h in range(N_HEADS):
                sl = slice(h * HEAD_DIM, (h + 1) * HEAD_DIM)
                sc = _dot_nt(q_ref[r0:r0 + tq, sl], window(kp_ref, kc_ref, kn_ref, gg, j, sl))
                items.append(dict(gg=gg, j=j, h=h, sl=sl, r0=r0, valid=valid, sc=sc))
        for it in items:
            slope = 2.0 ** (-8.0 * (it["h"] + 1) / N_HEADS)
            sc = jnp.where(it["valid"], it["sc"] * scale - slope * absrel, NEG_INF)
            m = jnp.max(sc, axis=-1, keepdims=True)
            p = jnp.exp(sc - m)
            den = jnp.sum(p, axis=-1, keepdims=True)
            it["p"], it["den"], it["lse"] = p.astype(BF16), den, m + jnp.log(den)
        for it in items:
            pv = _dot(it["p"], window(vp_ref, vc_ref, vn_ref, it["gg"], it["j"], it["sl"]))
            o_scr[it["h"], it["r0"]:it["r0"] + tq, :] = pv / it["den"]
        for gg, j in group:
            lse_tile = jnp.zeros((tq, LANES), F32)
            for it in items:
                if (it["gg"], it["j"]) == (gg, j):
                    lse_tile = jnp.where(lane == it["h"], it["lse"], lse_tile)
            r0 = gg * w + j * tq
            l_scr[r0:r0 + tq, :] = lse_tile
    step = ATTN_PERM // dilation
    nblk = o_ref.shape[0]
    for blk in range(nblk):
        if step == 1:
            rows = slice(blk * half, (blk + 1) * half)
        else:
            rows = pl.ds(blk, half, stride=step)
        for h in range(N_HEADS):
            o_ref[blk, :, h * HEAD_DIM:(h + 1) * HEAD_DIM] = o_scr[h, rows, :].astype(o_ref.dtype)
        lse_ref[blk] = l_scr[rows, :]


def dilated_attention(proj, gi, dilation):
    b, s, _ = proj.shape
    d = D_MODEL
    half = ATTN_HALF
    tp = ATTN_TP
    assert s % tp == 0 and tp % (dilation * half) == 0 and ATTN_PERM % dilation == 0
    nt = s // tp
    w = tp // dilation
    g = 1 if w > half else ATTN_RES_PER_STEP
    nres = dilation // g
    base = gi * 3
    hpt = tp // half

    def cur(which):
        return pl.BlockSpec((None, g * w, d), lambda bi, t, r: (bi, t * nres + r, base + which))

    if g == 1:
        wh = w // half

        def prev(which):
            return pl.BlockSpec((None, half, d),
                                lambda bi, t, r: (bi, jnp.maximum((t - 1) * hpt + (r + 1) * wh - 1, 0), base + which))

        def nxt(which):
            return pl.BlockSpec((None, half, d),
                                lambda bi, t, r: (bi, jnp.minimum(t + 1, nt - 1) * hpt + r * wh, base + which))
    else:
        def prev(which):
            return pl.BlockSpec((None, g * w, d), lambda bi, t, r: (bi, jnp.maximum(t - 1, 0) * nres + r, base + which))

        def nxt(which):
            return pl.BlockSpec((None, g * w, d),
                                lambda bi, t, r: (bi, jnp.minimum(t + 1, nt - 1) * nres + r, base + which))

    perm = ATTN_PERM
    if dilation == perm:
        nblk, lead = g, (b, nt, perm)
        omap = lambda bi, t, r: (bi, t, r, 0, 0)
        oblock = lambda last: (None, None, nblk, half, last)
    else:
        nblk, lead = perm // dilation, (b, nt, perm // dilation, dilation)
        omap = lambda bi, t, r: (bi, t, 0, r, 0, 0)
        oblock = lambda last: (None, None, nblk, None, half, last)

    o, lse = pl.pallas_call(
        functools.partial(_attn_kernel, dilation=dilation, w=w, g=g, sub_len=s // dilation),
        grid=(b, nt, nres),
        in_specs=[cur(0), prev(1), cur(1), nxt(1), prev(2), cur(2), nxt(2)],
        out_specs=[pl.BlockSpec(oblock(d), omap), pl.BlockSpec(oblock(LANES), omap)],
        out_shape=[jax.ShapeDtypeStruct(lead + (half, d), BF16), jax.ShapeDtypeStruct(lead + (half, LANES), F32)],
        scratch_shapes=[pltpu.VMEM((N_HEADS, g * w, HEAD_DIM), F32), pltpu.VMEM((g * w, LANES), F32)],
        compiler_params=_cparams("parallel", "parallel", "parallel"),
        name=f"dilated_attn_g{gi}",
    )(proj, proj, proj, proj, proj, proj, proj)
    return o.reshape(b * s, d), lse.reshape(b * s, LANES)


def _attn_out_kernel(o0_ref, o1_ref, o2_ref, l0_ref, l1_ref, l2_ref, wo_ref, x_ref, out_ref, y_scr):
    parts = []
    for h in range(N_HEADS):
        sl = slice(h * HEAD_DIM, (h + 1) * HEAD_DIM)
        a0, a1, a2 = l0_ref[:, h:h + 1], l1_ref[:, h:h + 1], l2_ref[:, h:h + 1]
        mx = jnp.maximum(jnp.maximum(a0, a1), a2)
        e0, e1, e2 = jnp.exp(a0 - mx), jnp.exp(a1 - mx), jnp.exp(a2 - mx)
        o = (e0 * o0_ref[:, sl].astype(F32) + e1 * o1_ref[:, sl].astype(F32) + e2 * o2_ref[:, sl].astype(F32))
        parts.append((o / (e0 + e1 + e2)).astype(BF16))
    y = _dot(jnp.concatenate(parts, axis=1), wo_ref[...])
    ncol = y_scr.shape[0]
    for c in range(ncol):
        y_scr[c] = y[:, c * LANES:(c + 1) * LANES]
    per = ATTN_TP // ATTN_PERM
    for l in range(per):
        rows = slice(l * ATTN_PERM, (l + 1) * ATTN_PERM)
        for c in range(ncol):
            cols = slice(c * LANES, (c + 1) * LANES)
            out_ref[rows, cols] = x_ref[rows, cols] + y_scr[c, pl.ds(l, ATTN_PERM, stride=per), :]


def attn_out(os, lses, w_out, x):
    n, d = x.shape
    tm = ATTN_TP
    tile = lambda i: (i, 0)
    big = pl.BlockSpec((tm, d), tile)
    small = pl.BlockSpec((tm, LANES), tile)
    return pl.pallas_call(
        _attn_out_kernel,
        grid=(n // tm,),
        in_specs=[big, big, big, small, small, small, pl.BlockSpec((d, d), lambda i: (0, 0)), big],
        out_specs=big,
        out_shape=jax.ShapeDtypeStruct((n, d), F32),
        scratch_shapes=[pltpu.VMEM((d // LANES, tm, LANES), F32)],
        compiler_params=_cparams("parallel"),
        name="attn_out",
    )(*os, *lses, w_out, x)


def _attn_proj_kernel(x_ref, g_ref, w_ref, o_ref, h_ref, acc_ref, *, blocks_per_group):
    j = pl.program_id(1)

    @pl.when(j == 0)
    def _():
        h_ref[...] = _rms(x_ref[...], g_ref[...]).astype(BF16)

    res = _dot(h_ref[...], w_ref[...])
    ncol = acc_ref.shape[0]
    for c in range(ncol):
        acc_ref[c] = res[:, c * LANES:(c + 1) * LANES]
    gi = j // blocks_per_group
    for gidx, (_, dilation) in enumerate(ATTN_GROUPS):
        @pl.when(gi == gidx)
        def _():
            if dilation == 1:
                o_ref[...] = res.astype(o_ref.dtype)
            else:
                w = ATTN_TP // dilation
                for r in range(dilation):
                    for c in range(ncol):
                        o_ref[r * w:(r + 1) * w, c * LANES:(c + 1) * LANES] = (
                            acc_ref[c, pl.ds(r, w, stride=dilation), :].astype(o_ref.dtype))


def attn_proj(x, g, w, tn=1024):
    n, d = x.shape
    nout = w.shape[1]
    tm = ATTN_TP
    group_cols = nout // len(ATTN_GROUPS)
    assert group_cols % tn == 0
    return pl.pallas_call(
        functools.partial(_attn_proj_kernel, blocks_per_group=group_cols // tn),
        grid=(n // tm, nout // tn),
        in_specs=[pl.BlockSpec((tm, d), lambda i, j: (i, 0)),
                  pl.BlockSpec((1, d), lambda i, j: (0, 0)),
                  pl.BlockSpec((d, tn), lambda i, j: (0, j))],
        out_specs=pl.BlockSpec((tm, tn), lambda i, j: (i, j)),
        out_shape=jax.ShapeDtypeStruct((n, nout), BF16),
        scratch_shapes=[pltpu.VMEM((tm, d), BF16), pltpu.VMEM((tn // LANES, tm, LANES), F32)],
        compiler_params=_cparams("parallel", "arbitrary"),
        name="attn_proj",
    )(x, g.reshape(1, d), w)


def _route_kernel(x_ref, g_ref, r_ref, h_ref, aff_ref):
    h = _rms(x_ref[...], g_ref[...])
    h_ref[...] = h.astype(BF16)
    logits = jnp.dot(h, r_ref[...], precision=lax.Precision.HIGHEST, preferred_element_type=F32)
    lt = logits.T[:N_EXPERTS, :]
    m = jnp.max(lt, axis=0, keepdims=True)
    e = jnp.exp(lt - m)
    aff_ref[...] = e / jnp.sum(e, axis=0, keepdims=True)


def moe_route(x, g, router, tm=512):
    n, d = x.shape
    rpad = jnp.zeros((d, LANES), F32).at[:, :N_EXPERTS].set(router)
    return pl.pallas_call(
        _route_kernel,
        grid=(n // tm,),
        in_specs=[pl.BlockSpec((tm, d), lambda i: (i, 0)),
                  pl.BlockSpec((1, d), lambda i: (0, 0)),
                  pl.BlockSpec((d, LANES), lambda i: (0, 0))],
        out_specs=[pl.BlockSpec((tm, d), lambda i: (i, 0)),
                   pl.BlockSpec((N_EXPERTS, tm), lambda i: (0, i))],
        out_shape=[jax.ShapeDtypeStruct((n, d), BF16), jax.ShapeDtypeStruct((N_EXPERTS, n), F32)],
        compiler_params=_cparams("parallel"),
        name="moe_route",
    )(x, g.reshape(1, d), rpad)


def _select_kernel(aff_ref, slot_ref, tab_ref, *, cap):
    e, n = aff_ref.shape
    bits = pltpu.bitcast(aff_ref[...], I32)
    tidx = lax.broadcasted_iota(I32, (e, n), 1)

    def count(mask_i32):
        return jnp.sum(mask_i32.astype(F32), axis=1, keepdims=True).astype(I32)

    def thr_body(it, thr):
        cand = thr | jnp.left_shift(jnp.int32(1), 30 - it)
        cnt = count(jnp.where(bits >= cand, 1, 0))
        return jnp.where(cnt >= cap, cand, thr)

    thr = lax.fori_loop(0, 31, thr_body, jnp.zeros((e, 1), I32))
    gt = jnp.where(bits > thr, 1, 0)
    eq = jnp.where(bits == thr, 1, 0)
    need = cap - count(gt)
    nbits = max(1, (n - 1).bit_length())

    def tie_body(it, j):
        cand = j | jnp.left_shift(jnp.int32(1), nbits - 1 - it)
        cnt = count(jnp.where(tidx < cand, eq, 0))
        return jnp.where(cnt < need, cand, j)

    jlast = lax.fori_loop(0, nbits, tie_body, jnp.zeros((e, 1), I32))
    sel = gt + jnp.where(tidx <= jlast, eq, 0)
    slot_ref[...] = sel
    rr = lax.broadcasted_iota(I32, (LANES, LANES), 0)
    cc = lax.broadcasted_iota(I32, (LANES, LANES), 1)
    upper = jnp.where(rr <= cc, 1.0, 0.0).astype(BF16)

    nblk = n // LANES
    blk_lane = lax.broadcasted_iota(I32, (e, nblk), 1)

    tab_ref[...] = jnp.zeros((e, nblk), I32)

    def blk_body(bi, run):
        off = pl.multiple_of(bi * LANES, LANES)
        sb = slot_ref[:, pl.ds(off, LANES)]
        inc = _dot(sb.astype(F32).astype(BF16), upper).astype(I32)
        slot_ref[:, pl.ds(off, LANES)] = jnp.where(sb > 0, inc - sb + run, -1)
        tab_ref[...] = jnp.where(blk_lane == bi, run, tab_ref[...])
        return run + inc[:, LANES - 1:LANES]

    lax.fori_loop(0, nblk, blk_body, jnp.zeros((e, 1), I32))


def moe_select(aff_t, cap):
    e, n = aff_t.shape
    return pl.pallas_call(
        functools.partial(_select_kernel, cap=cap),
        grid=(1,),
        in_specs=[pl.BlockSpec((e, n), lambda i: (0, 0))],
        out_specs=[pl.BlockSpec((e, n), lambda i: (0, 0)), pl.BlockSpec((e, n // LANES), lambda i: (0, 0))],
        out_shape=[jax.ShapeDtypeStruct((e, n), I32), jax.ShapeDtypeStruct((e, n // LANES), I32)],
        compiler_params=_cparams("arbitrary"),
        name="moe_select",
    )(aff_t)


def _ffn_kernel(x_ref, wg_ref, wu_ref, wd_ref, o_ref):
    x = x_ref[...]
    acc = None
    for f in range(EXPERT_FF // FF_CHUNK):
        fs = slice(f * FF_CHUNK, (f + 1) * FF_CHUNK)
        hid = (_silu(_dot(x, wg_ref[:, fs])) * _dot(x, wu_ref[:, fs])).astype(BF16)
        part = _dot(hid, wd_ref[fs, :])
        acc = part if acc is None else acc + part
    o_ref[...] = acc.astype(o_ref.dtype)


def moe_ffn(xe, w_gate, w_up, w_down, layer, tm=1024):
    e, c, d = xe.shape
    f = w_gate.shape[3]
    return pl.pallas_call(
        _ffn_kernel,
        grid=(e, c // tm),
        in_specs=[pl.BlockSpec((None, tm, d), lambda ei, j: (ei, j, 0)),
                  pl.BlockSpec((None, None, d, f), lambda ei, j: (layer, ei, 0, 0)),
                  pl.BlockSpec((None, None, d, f), lambda ei, j: (layer, ei, 0, 0)),
                  pl.BlockSpec((None, None, f, d), lambda ei, j: (layer, ei, 0, 0))],
        out_specs=pl.BlockSpec((None, tm, d), lambda ei, j: (ei, j, 0)),
        out_shape=jax.ShapeDtypeStruct((e, c, d), BF16),
        compiler_params=_cparams("parallel", "arbitrary"),
        name="moe_ffn",
    )(xe, w_gate, w_up, w_down)


def _slot_ranges(tab_ref, i, cap):
    out = []
    for e in range(N_EXPERTS):
        p0 = tab_ref[i * N_EXPERTS + e]
        out.append((e * cap + p0, tab_ref[(i + 1) * N_EXPERTS + e] - p0))
    return out


def _dispatch_kernel(tab_ref, h_ref, slot_ref, xe_ref, stage_ref, tail_ref, sem, npiece_ref, *, cap, tile):
    i = pl.program_id(0)
    last = pl.num_programs(0) - 1
    slot = i % 2
    grp = BF16_SUBLANES
    kc = COMBINE_KCHUNK

    def drain(sl):
        def body(_, carry):
            pltpu.make_async_copy(stage_ref.at[sl, pl.ds(0, grp)], xe_ref.at[pl.ds(0, grp)], sem.at[sl]).wait()
            return carry
        lax.fori_loop(0, npiece_ref[sl], body, 0)

    @pl.when(i >= 2)
    def _():
        drain(slot)

    meta, bases = [], []
    off = jnp.int32(0)
    for g0, cnt in _slot_ranges(tab_ref, i, cap):
        a = (g0 // grp) * grp
        lead = g0 - a
        tot = lead + cnt
        meta.append((off, a, lead, tot, (tot // grp) * grp))
        bases.append(off - a)
        off = off + ((tot + grp - 1) // grp) * grp
    row_e = lax.broadcasted_iota(I32, (N_EXPERTS, 1), 0)
    cvec = jnp.zeros((N_EXPERTS, 1), I32)
    for e in range(N_EXPERTS):
        cvec = jnp.where(row_e == e, bases[e] + e * cap, cvec)
    slot_e = slot_ref[...]
    col = jnp.where(slot_e >= 0, slot_e + cvec, -1)

    def chunk_body(ci, carry):
        base = pl.multiple_of(ci * kc, kc)
        rows = lax.broadcasted_iota(I32, (kc, tile), 0) + base
        p = jnp.zeros((kc, tile), F32)
        for e in range(N_EXPERTS):
            p = jnp.where(col[e:e + 1, :] == rows, 1.0, p)
        stage_ref[slot, pl.ds(base, kc), :] = _dot(p.astype(BF16), h_ref[...]).astype(stage_ref.dtype)
        return carry

    lax.fori_loop(0, (off + kc - 1) // kc, chunk_body, 0)

    row_g = lax.broadcasted_iota(I32, (grp, D_MODEL), 0)
    npiece = jnp.int32(0)
    for e, (off_e, a, lead, tot, nfull) in enumerate(meta):
        first = pl.ds(pl.multiple_of(off_e, grp), grp)

        @pl.when(lead > 0)
        def _():
            stage_ref[slot, first, :] = jnp.where(row_g < lead, tail_ref[e], stage_ref[slot, first, :])

        @pl.when(tot > nfull)
        def _():
            tail_ref[e] = stage_ref[slot, pl.ds(pl.multiple_of(off_e + nfull, grp), grp), :]

        def piece(k, carry):
            pltpu.make_async_copy(stage_ref.at[slot, pl.ds(pl.multiple_of(off_e + k * grp, grp), grp)],
                                  xe_ref.at[pl.ds(pl.multiple_of(a + k * grp, grp), grp)], sem.at[slot]).start()
            return carry

        lax.fori_loop(0, nfull // grp, piece, 0)
        npiece = npiece + nfull // grp
    npiece_ref[slot] = npiece

    @pl.when(i == last)
    def _():
        drain(slot)

        @pl.when(i >= 1)
        def _():
            drain(1 - slot)


def moe_dispatch(hb, slot, tab, cap, tile=COMBINE_TILE):
    n, d = hb.shape
    kmax = N_EXPERTS * (tile + BF16_SUBLANES)
    kmax = -(-kmax // COMBINE_KCHUNK) * COMBINE_KCHUNK
    grid_spec = pltpu.PrefetchScalarGridSpec(
        num_scalar_prefetch=1,
        grid=(n // tile,),
        in_specs=[pl.BlockSpec((tile, d), lambda i, tab: (i, 0)),
                  pl.BlockSpec((N_EXPERTS, tile), lambda i, tab: (0, i))],
        out_specs=pl.BlockSpec(memory_space=pl.ANY),
        scratch_shapes=[pltpu.VMEM((2, kmax, d), BF16), pltpu.VMEM((N_EXPERTS, BF16_SUBLANES, d), BF16),
                        pltpu.SemaphoreType.DMA((2,)), pltpu.SMEM((2,), I32)],
    )
    return pl.pallas_call(
        functools.partial(_dispatch_kernel, cap=cap, tile=tile),
        grid_spec=grid_spec,
        out_shape=jax.ShapeDtypeStruct((N_EXPERTS * cap, d), BF16),
        compiler_params=_cparams("arbitrary"),
        name="moe_dispatch",
    )(tab, hb, slot)


def _combine_ranges(tab_ref, i, cap):
    grp = BF16_SUBLANES
    slabs, bases = [], []
    off = jnp.int32(0)
    for e, (g0, cnt) in enumerate(_slot_ranges(tab_ref, i, cap)):
        a = (g0 // grp) * grp
        nrow = jnp.where(cnt > 0, ((g0 - a + cnt + grp - 1) // grp) * grp, 0)
        slabs.append((off, a, nrow))
        bases.append(off - a + e * cap)
        off = off + nrow
    return slabs, bases, off


def _combine_copies(tab_ref, ye_ref, zbuf_ref, sem, i, cap, tile):
    grp = BF16_SUBLANES
    max_shift = (tile // grp + 1).bit_length() - 1
    copies = []
    for off_e, a, nrow in _combine_ranges(tab_ref, i, cap)[0]:
        src, dst = a, off_e
        for sh in range(max_shift, -1, -1):
            size = grp << sh
            take = (nrow & size) != 0
            copies.append((take, pltpu.make_async_copy(ye_ref.at[pl.ds(pl.multiple_of(src, grp), size)],
                                                       zbuf_ref.at[pl.ds(pl.multiple_of(dst, grp), size)], sem)))
            step = jnp.where(take, size, 0)
            src, dst = src + step, dst + step
    return copies


def _combine_kernel(tab_ref, x_ref, slot_ref, aff_ref, ye_ref, g_ref, o_ref, zbuf_ref, sem,
                    *, cap, tile, final_norm):
    i = pl.program_id(0)
    slot = i % 2

    def start_tile(ti, sl):
        for take, cp in _combine_copies(tab_ref, ye_ref, zbuf_ref.at[sl], sem.at[sl], ti, cap, tile):
            @pl.when(take)
            def _():
                cp.start()

    @pl.when(i == 0)
    def _():
        zbuf_ref[...] = jnp.zeros_like(zbuf_ref)
        start_tile(0, 0)

    @pl.when(i + 1 < pl.num_programs(0))
    def _():
        start_tile(i + 1, 1 - slot)

    zcur_ref = zbuf_ref.at[slot]
    _, bases, k_tot = _combine_ranges(tab_ref, i, cap)
    row_e = lax.broadcasted_iota(I32, (N_EXPERTS, 1), 0)
    cvec = jnp.zeros((N_EXPERTS, 1), I32)
    for e in range(N_EXPERTS):
        cvec = jnp.where(row_e == e, bases[e], cvec)
    slot_e = slot_ref[...]
    col = jnp.where(slot_e >= 0, slot_e + cvec, -1)
    aff = aff_ref[...]
    o_ref[...] = x_ref[...]

    for take, cp in _combine_copies(tab_ref, ye_ref, zcur_ref, sem.at[slot], i, cap, tile):
        @pl.when(take)
        def _():
            cp.wait()

    kc = COMBINE_KCHUNK

    def chunk_body(ci, carry):
        base = pl.multiple_of(ci * kc, kc)
        z = zcur_ref[pl.ds(base, kc), :]
        rows = lax.broadcasted_iota(I32, (kc, tile), 0) + base
        p = jnp.zeros((kc, tile), F32)
        for e in range(N_EXPERTS):
            p = jnp.where(col[e:e + 1, :] == rows, aff[e:e + 1, :], p)
        p_hi = p.astype(BF16)
        p_lo = (p - p_hi.astype(F32)).astype(BF16)
        o_ref[...] += _dot_tn(p_hi, z) + _dot_tn(p_lo, z)
        return carry

    lax.fori_loop(0, (k_tot + kc - 1) // kc, chunk_body, 0)
    if final_norm:
        o_ref[...] = _rms(o_ref[...], g_ref[...])


def moe_combine(x, slot, aff, tab, ye, cap, norm_g=None, tile=COMBINE_TILE):
    n, d = x.shape
    kmax = N_EXPERTS * (tile + BF16_SUBLANES)
    kmax = -(-kmax // COMBINE_KCHUNK) * COMBINE_KCHUNK
    final_norm = norm_g is not None
    g = (norm_g if final_norm else jnp.ones((d,), F32)).reshape(1, d)
    grid_spec = pltpu.PrefetchScalarGridSpec(
        num_scalar_prefetch=1,
        grid=(n // tile,),
        in_specs=[pl.BlockSpec((tile, d), lambda i, tab: (i, 0)),
                  pl.BlockSpec((N_EXPERTS, tile), lambda i, tab: (0, i)),
                  pl.BlockSpec((N_EXPERTS, tile), lambda i, tab: (0, i)),
                  pl.BlockSpec(memory_space=pl.ANY),
                  pl.BlockSpec((1, d), lambda i, tab: (0, 0))],
        out_specs=pl.BlockSpec((tile, d), lambda i, tab: (i, 0)),
        scratch_shapes=[pltpu.VMEM((2, kmax, d), BF16), pltpu.SemaphoreType.DMA((2,))],
    )
    return pl.pallas_call(
        functools.partial(_combine_kernel, cap=cap, tile=tile, final_norm=final_norm),
        grid_spec=grid_spec,
        out_shape=jax.ShapeDtypeStruct((n, d), F32),
        compiler_params=_cparams("arbitrary"),
        name="moe_combine",
    )(tab, x, slot, aff, ye, g)


def expert_choice_moe(x, norm_g, router, w_gate, w_up, w_down, layer, final_g=None):
    n, d = x.shape
    cap = max(1, CAPACITY_FACTOR * n // N_EXPERTS)
    hb, aff_t = moe_route(x, norm_g, router)
    slot, blk_tab = moe_select(aff_t, cap)
    tab = jnp.concatenate([blk_tab[:, ::COMBINE_TILE // LANES].T, jnp.full((1, N_EXPERTS), cap, I32)], axis=0)
    tab = tab.reshape(-1)
    xe = moe_dispatch(hb, slot, tab, cap).reshape(N_EXPERTS, cap, d)
    ye = moe_ffn(xe, w_gate, w_up, w_down, layer).reshape(N_EXPERTS * cap, d)
    return moe_combine(x, slot, aff_t, tab, ye, cap, norm_g=final_g)


def _trunk(x, p):
    b, s, d = x.shape
    n = b * s
    x = x.reshape(n, d)
    for i in range(DEPTH):
        kind, j = i % N_MIXERS, i // N_MIXERS
        if kind == 0:
            u = norm_glu(x, p["norm_mix"][i], p["conv_w_glu"][j], p["conv_b_glu"][j])
            x = conv_out(u.reshape(b, s, d), x.reshape(b, s, d), p["conv_w_dw"][j], p["conv_b_dw"][j],
                         p["conv_ln_g"][j], p["conv_ln_b"][j], p["conv_w_out"][j], p["conv_b_out"][j]).reshape(n, d)
        elif kind == 1:
            proj = norm_linear(x, p["norm_mix"][i], p["dn_w_in"][j], F32, tn=768)
            q, k, v, bg = dn_prep(proj.reshape(b, s, -1), p["dn_w_conv"][j], p["dn_a_log"][j], p["dn_dt_bias"][j])
            o_f, o_b = delta_rule(q, k, v, bg)
            x = dn_out(o_f.reshape(n, d), o_b.reshape(n, d), proj, p["dn_norm_g"][j], p["dn_w_out"][j], x)
        else:
            proj = attn_proj(x, p["norm_mix"][i], p["attn_w_in"][j]).reshape(b, s, -1)
            os, lses = [], []
            for gi, (window, dilation) in enumerate(ATTN_GROUPS):
                assert window // (2 * dilation) == ATTN_HALF
                o, lse = dilated_attention(proj, gi, dilation)
                os.append(o)
                lses.append(lse)
            x = attn_out(os, lses, p["attn_w_out"][j], x)
        x = expert_choice_moe(x, p["norm_ffn"][i], p["moe_router"][i], p["moe_w_gate"], p["moe_w_up"],
                              p["moe_w_down"], i, final_g=p["norm_final"] if i == DEPTH - 1 else None)
    return x.reshape(b, s, d)


def kernel(x_prompt, x_sample, norm_mix, norm_ffn, norm_final, conv_w_glu, conv_b_glu, conv_w_dw, conv_b_dw,
           conv_ln_g, conv_ln_b, conv_w_out, conv_b_out, dn_w_in, dn_w_conv, dn_a_log, dn_dt_bias, dn_norm_g,
           dn_w_out, attn_w_in, attn_w_out, moe_router, moe_w_gate, moe_w_up, moe_w_down):
    dn_pad = (-dn_w_in.shape[-1]) % 512
    p = dict(
        norm_mix=norm_mix, norm_ffn=norm_ffn, norm_final=norm_final,
        conv_w_glu=conv_w_glu.astype(BF16), conv_b_glu=conv_b_glu, conv_w_dw=conv_w_dw, conv_b_dw=conv_b_dw,
        conv_ln_g=conv_ln_g, conv_ln_b=conv_ln_b, conv_w_out=conv_w_out.astype(BF16), conv_b_out=conv_b_out,
        dn_w_in=jnp.pad(dn_w_in, ((0, 0), (0, 0), (0, dn_pad))).astype(BF16), dn_w_conv=dn_w_conv,
        dn_a_log=dn_a_log, dn_dt_bias=dn_dt_bias, dn_norm_g=dn_norm_g, dn_w_out=dn_w_out.astype(BF16),
        attn_w_in=attn_w_in.astype(BF16), attn_w_out=attn_w_out.astype(BF16),
        moe_router=moe_router, moe_w_gate=moe_w_gate.astype(BF16), moe_w_up=moe_w_up.astype(BF16),
        moe_w_down=moe_w_down.astype(BF16),
    )
    return (_trunk(x_prompt, p), _trunk(x_sample, p))
```
